```python
import math
import jax, jax.numpy as jnp
from jax import lax
import numpy as np

D_MODEL = 1024
BATCH = 16
SEQ = 2048
DEPTH = 4

CHUNK = 64
N_META = 16
D_MIX = D_MODEL
SB_HEAD_DIM = 64
SB_WIDTH = D_MIX // 2
SB_HEADS = SB_WIDTH // SB_HEAD_DIM
CONF_WIDTH = D_MIX // 4
CONF_KERNEL = 31
SC_WIDTH = D_MIX - SB_WIDTH - CONF_WIDTH
SC_KERNEL = 3
D_IN = 3 * SB_WIDTH + 2 * CONF_WIDTH + 3 * SC_WIDTH
D_FF = 4 * D_MODEL
QBLOCK = 128
DEEPNORM_ALPHA = (2.0 * DEPTH) ** 0.25
DEEPNORM_BETA = (8.0 * DEPTH) ** -0.25
LN_EPS = 1e-5
RMS_EPS = 1e-6

kernel_name = "hymba_stickbreak_conformer_shortconv_deepnorm"


def layer_norm(x, g, b):
    xf = x.astype(jnp.float32)
    mu = jnp.mean(xf, axis=-1, keepdims=True)
    var = jnp.mean(jnp.square(xf - mu), axis=-1, keepdims=True)
    out = (xf - mu) * lax.rsqrt(var + LN_EPS) * g.astype(jnp.float32) + b.astype(jnp.float32)
    return out.astype(x.dtype)


def rms_norm(x, g):
    xf = x.astype(jnp.float32)
    out = xf * lax.rsqrt(jnp.mean(jnp.square(xf), axis=-1, keepdims=True) + RMS_EPS) * g.astype(jnp.float32)
    return out.astype(x.dtype)


def causal_depthwise_conv(x, w):
    k_width, ch = w.shape
    return lax.conv_general_dilated(
        x, w[:, None, :].astype(x.dtype), window_strides=(1,), padding=[(k_width - 1, 0)],
        dimension_numbers=("NWC", "WIO", "NWC"), feature_group_count=ch)


def stick_breaking_attention(q, k, v):
    bsz, n_heads, length, dh = q.shape
    n_blk = -(-length // QBLOCK)
    pad = ((0, 0), (0, 0), (0, n_blk * QBLOCK - length), (0, 0))
    q, k, v = jnp.pad(q, pad), jnp.pad(k, pad), jnp.pad(v, pad)
    scale = dh ** -0.5
    outs = []
    for i in range(n_blk):
        q0, kend = i * QBLOCK, (i + 1) * QBLOCK
        qb = q[:, :, q0:kend].astype(jnp.float32)
        kb = k[:, :, :kend].astype(jnp.float32)
        vb = v[:, :, :kend].astype(jnp.float32)
        z = jnp.einsum("bhtd,bhsd->bhts", qb, kb) * scale
        t_idx = q0 + jnp.arange(QBLOCK)[:, None]
        s_idx = jnp.arange(kend)[None, :]
        past = s_idx < t_idx
        log_keep = jnp.where(past, -jax.nn.softplus(z), 0.0)
        between = lax.cumsum(log_keep, axis=3, reverse=True) - log_keep
        log_a = jax.nn.log_sigmoid(z) + between
        a = jnp.where(past, jnp.exp(log_a), 0.0)
        outs.append(jnp.einsum("bhts,bhsd->bhtd", a, vb))
    o = jnp.concatenate(outs, axis=2)[:, :, :length]
    return o.astype(v.dtype)


def conformer_conv(u, w_dw, b_dw, ln_g, ln_b):
    a, gate = jnp.split(u, 2, axis=-1)
    h = a * jax.nn.sigmoid(gate)
    h = causal_depthwise_conv(h, w_dw) + b_dw.astype(h.dtype)
    h = layer_norm(h, ln_g, ln_b)
    return jax.nn.swish(h)


def short_gated_conv(u, w_dw):
    b_gate, c_gate, h = jnp.split(u, 3, axis=-1)
    return b_gate * causal_depthwise_conv(c_gate * h, w_dw)


def hybrid_mixer(h, w_in, w_conf_dw, b_conf_dw, ln_conf_g, ln_conf_b, w_short_dw, g_mix, w_out):
    bsz, length, _ = h.shape
    u = h @ w_in
    q, k, v, conf_in, sc_in = jnp.split(
        u, [SB_WIDTH, 2 * SB_WIDTH, 3 * SB_WIDTH, 3 * SB_WIDTH + 2 * CONF_WIDTH], axis=-1)

    def to_heads(t):
        return t.reshape(bsz, length, SB_HEADS, SB_HEAD_DIM).transpose(0, 2, 1, 3)

    o_sb = stick_breaking_attention(to_heads(q), to_heads(k), to_heads(v))
    o_sb = o_sb.transpose(0, 2, 1, 3).reshape(bsz, length, SB_WIDTH)
    o_conf = conformer_conv(conf_in, w_conf_dw, b_conf_dw, ln_conf_g, ln_conf_b)
    o_sc = short_gated_conv(sc_in, w_short_dw)
    g_sb, g_conf, g_sc = jnp.split(g_mix, [SB_WIDTH, SB_WIDTH + CONF_WIDTH])
    y = jnp.concatenate([rms_norm(o_sb, g_sb), rms_norm(o_conf, g_conf), rms_norm(o_sc, g_sc)], axis=-1)
    return y @ w_out


def squared_relu_mlp(h, w1, w2):
    return jnp.square(jax.nn.relu(h @ w1)) @ w2


def setup_inputs(seed: int = 0) -> dict:
    key = jax.random.key(seed)
    ks = jax.random.split(key, 20)
    f32 = jnp.float32

    def nrm(k, shape, scale):
        return jax.random.normal(k, shape, f32) * scale

    return {
        "x": nrm(ks[0], (BATCH, SEQ, D_MODEL), 1.0),
        "meta_tokens": nrm(ks[1], (N_META, D_MODEL), 1.0),
        "ln_in_g": 1.0 + nrm(ks[2], (D_MODEL,), 0.01),
        "ln_in_b": nrm(ks[3], (D_MODEL,), 0.01),
        "w_in": nrm(ks[4], (DEPTH, D_MODEL, D_IN), D_MODEL ** -0.5),
        "w_conf_dw": nrm(ks[5], (DEPTH, CONF_KERNEL, CONF_WIDTH), CONF_KERNEL ** -0.5),
        "b_conf_dw": nrm(ks[6], (DEPTH, CONF_WIDTH), 0.01),
        "ln_conf_g": 1.0 + nrm(ks[7], (DEPTH, CONF_WIDTH), 0.01),
        "ln_conf_b": nrm(ks[8], (DEPTH, CONF_WIDTH), 0.01),
        "w_short_dw": nrm(ks[9], (DEPTH, SC_KERNEL, SC_WIDTH), SC_KERNEL ** -0.5),
        "g_mix": 1.0 + nrm(ks[10], (DEPTH, D_MIX), 0.01),
        "w_out": nrm(ks[11], (DEPTH, D_MIX, D_MODEL), D_MIX ** -0.5 * DEEPNORM_BETA),
        "ln_mix_g": 1.0 + nrm(ks[12], (DEPTH, D_MODEL), 0.01),
        "ln_mix_b": nrm(ks[13], (DEPTH, D_MODEL), 0.01),
        "w_ff1": nrm(ks[14], (DEPTH, D_MODEL, D_FF), D_MODEL ** -0.5),
        "w_ff2": nrm(ks[15], (DEPTH, D_FF, D_MODEL), D_FF ** -0.5 * DEEPNORM_BETA),
        "ln_ff_g": 1.0 + nrm(ks[16], (DEPTH, D_MODEL), 0.01),
        "ln_ff_b": nrm(ks[17], (DEPTH, D_MODEL), 0.01),
    }


def reference(x, meta_tokens, ln_in_g, ln_in_b, w_in, w_conf_dw, b_conf_dw, ln_conf_g, ln_conf_b,
              w_short_dw, g_mix, w_out, ln_mix_g, ln_mix_b, w_ff1, w_ff2, ln_ff_g, ln_ff_b):
    bsz = x.shape[0]
    meta = jnp.broadcast_to(meta_tokens[None].astype(x.dtype), (bsz, N_META, D_MODEL))
    h = layer_norm(jnp.concatenate([meta, x], axis=1), ln_in_g, ln_in_b)
    for l in range(DEPTH):
        mix = hybrid_mixer(h, w_in[l], w_conf_dw[l], b_conf_dw[l], ln_conf_g[l], ln_conf_b[l],
                           w_short_dw[l], g_mix[l], w_out[l])
        h = layer_norm(DEEPNORM_ALPHA * h + mix, ln_mix_g[l], ln_mix_b[l])
        ff = squared_relu_mlp(h, w_ff1[l], w_ff2[l])
        h = layer_norm(DEEPNORM_ALPHA * h + ff, ln_ff_g[l], ln_ff_b[l])
    return h[:, N_META:]
```

```python
import functools
import math

import jax
import jax.numpy as jnp
from jax import lax
from jax.experimental import pallas as pl
from jax.experimental.pallas import tpu as pltpu

D_MODEL = 1024
DEPTH = 4
N_META = 16
SB_WIDTH = 512
HEAD_DIM = 64
CONF_WIDTH = 256
CONF_KERNEL = 31
SC_WIDTH = 256
SC_KERNEL = 3
D_FF = 4 * D_MODEL
ALPHA = (2.0 * DEPTH) ** 0.25
LN_EPS = 1e-5
RMS_EPS = 1e-6
LOG2E = math.log2(math.e)

LANES = 128
KV_TILE = 256
CONF_HALO = 32
SC_HALO = 8
META_ROWS = 256
VMEM_LIMIT = 56 * 1024 * 1024

BF16 = jnp.bfloat16
F32 = jnp.float32


def _dot(a, b):
    return jnp.dot(a, b, preferred_element_type=F32)


def _layer_norm(x, g, b):
    mu = jnp.mean(x, axis=-1, keepdims=True)
    xc = x - mu
    var = jnp.mean(xc * xc, axis=-1, keepdims=True)
    return xc * lax.rsqrt(var + LN_EPS) * g + b


def _rms_norm(x, g):
    return x * lax.rsqrt(jnp.mean(x * x, axis=-1, keepdims=True) + RMS_EPS) * g


def _const_spec(shape):
    zeros = (0,) * len(shape)
    return pl.BlockSpec(shape, lambda *_: zeros)


def _ln_kernel(x_ref, g_ref, b_ref, o_ref):
    o_ref[...] = _layer_norm(x_ref[...], g_ref[...], b_ref[...])


def _ln_rows(x2d, g, b, tm):
    rows = x2d.shape[0]
    return pl.pallas_call(
        _ln_kernel,
        grid=(rows // tm,),
        in_specs=[pl.BlockSpec((tm, D_MODEL), lambda i: (i, 0)),
                  _const_spec((1, D_MODEL)), _const_spec((1, D_MODEL))],
        out_specs=pl.BlockSpec((tm, D_MODEL), lambda i: (i, 0)),
        out_shape=jax.ShapeDtypeStruct((rows, D_MODEL), F32),
        compiler_params=pltpu.CompilerParams(dimension_semantics=("arbitrary",),
                                             vmem_limit_bytes=VMEM_LIMIT),
        name="ln_in",
    )(x2d, g, b)


def _proj_kernel(h_ref, wq_ref, wkt_ref, wv_ref, wconf_ref, wsc_ref, wdw_ref, bdw_ref, lng_ref,
                 lnb_ref, wsh_ref, gconf_ref, gsc_ref, ctail_ref, stail_ref,
                 q_ref, kt_ref, v_ref, yc_ref, ys_ref, ctail_out, stail_out,
                 xc_ref, xs_ref, *, tm, tail_at):
    i = pl.program_id(1)

    @pl.when(i == 0)
    def _():
        xc_ref[0:CONF_HALO, :] = ctail_ref[0]
        xs_ref[0:SC_HALO, :] = stail_ref[0]

    hb = h_ref[0].astype(BF16)
    q_ref[0] = _dot(hb, wq_ref[...]).astype(BF16)
    kt = lax.dot_general(wkt_ref[...], hb, (((1,), (1,)), ((), ())),
                         preferred_element_type=F32)
    for c in range(tm // KV_TILE):
        kt_ref[0, c] = kt[:, c * KV_TILE:(c + 1) * KV_TILE].astype(BF16)
    v_ref[0] = _dot(hb, wv_ref[...]).astype(BF16)

    uc = _dot(hb, wconf_ref[...])
    xc_ref[CONF_HALO:CONF_HALO + tm, :] = uc[:, :CONF_WIDTH] * jax.nn.sigmoid(uc[:, CONF_WIDTH:])
    acc = jnp.broadcast_to(bdw_ref[...], (tm, CONF_WIDTH))
    for k in range(CONF_KERNEL):
        off = CONF_HALO - (CONF_KERNEL - 1) + k
        acc = acc + wdw_ref[k:k + 1, :] * xc_ref[off:off + tm, :]
    y = _layer_norm(acc, lng_ref[...], lnb_ref[...])
    y = y * jax.nn.sigmoid(y)
    yc_ref[0] = _rms_norm(y, gconf_ref[...]).astype(BF16)
    ctail_out[0] = xc_ref[tail_at:tail_at + CONF_HALO, :]
    xc_ref[0:CONF_HALO, :] = xc_ref[tm:tm + CONF_HALO, :]

    us = _dot(hb, wsc_ref[...])
    xs_ref[SC_HALO:SC_HALO + tm, :] = us[:, SC_WIDTH:2 * SC_WIDTH] * us[:, 2 * SC_WIDTH:]
    conv = None
    for k in range(SC_KERNEL):
        off = SC_HALO - (SC_KERNEL - 1) + k
        term = wsh_ref[k:k + 1, :] * xs_ref[off:off + tm, :]
        conv = term if conv is None else conv + term
    ys_ref[0] = _rms_norm(us[:, :SC_WIDTH] * conv, gsc_ref[...]).astype(BF16)
    stail_out[0] = xs_ref[tail_at:tail_at + SC_HALO, :]
    xs_ref[0:SC_HALO, :] = xs_ref[tm:tm + SC_HALO, :]


def _proj(h, w, ctail, stail, *, tm, tail_at):
    bsz, length, _ = h.shape
    n_kv = length // KV_TILE
    row = lambda width: pl.BlockSpec((1, tm, width), lambda b, i: (b, i, 0))
    in_specs = [
        row(D_MODEL),
        _const_spec((D_MODEL, SB_WIDTH)), _const_spec((SB_WIDTH, D_MODEL)),
        _const_spec((D_MODEL, SB_WIDTH)), _const_spec((D_MODEL, 2 * CONF_WIDTH)),
        _const_spec((D_MODEL, 3 * SC_WIDTH)),
        _const_spec((CONF_KERNEL, CONF_WIDTH)), _const_spec((1, CONF_WIDTH)),
        _const_spec((1, CONF_WIDTH)), _const_spec((1, CONF_WIDTH)),
        _const_spec((SC_KERNEL, SC_WIDTH)), _const_spec((1, CONF_WIDTH)), _const_spec((1, SC_WIDTH)),
        _const_spec((1, CONF_HALO, CONF_WIDTH)), _const_spec((1, SC_HALO, SC_WIDTH)),
    ]
    out_specs = [
        row(SB_WIDTH),
        pl.BlockSpec((1, tm // KV_TILE, SB_WIDTH, KV_TILE), lambda b, i: (b, i, 0, 0)),
        row(SB_WIDTH), row(CONF_WIDTH), row(SC_WIDTH),
        pl.BlockSpec((1, CONF_HALO, CONF_WIDTH), lambda b, i: (b, 0, 0)),
        pl.BlockSpec((1, SC_HALO, SC_WIDTH), lambda b, i: (b, 0, 0)),
    ]
    out_shape = [
        jax.ShapeDtypeStruct((bsz, length, SB_WIDTH), BF16),
        jax.ShapeDtypeStruct((bsz, n_kv, SB_WIDTH, KV_TILE), BF16),
        jax.ShapeDtypeStruct((bsz, length, SB_WIDTH), BF16),
        jax.ShapeDtypeStruct((bsz, length, CONF_WIDTH), BF16),
        jax.ShapeDtypeStruct((bsz, length, SC_WIDTH), BF16),
        jax.ShapeDtypeStruct((bsz, CONF_HALO, CONF_WIDTH), F32),
        jax.ShapeDtypeStruct((bsz, SC_HALO, SC_WIDTH), F32),
    ]
    return pl.pallas_call(
        functools.partial(_proj_kernel, tm=tm, tail_at=tail_at),
        grid=(bsz, length // tm),
        in_specs=in_specs, out_specs=out_specs, out_shape=out_shape,
        scratch_shapes=[pltpu.VMEM((tm + CONF_HALO, CONF_WIDTH), F32),
                        pltpu.VMEM((tm + SC_HALO, SC_WIDTH), F32)],
        compiler_params=pltpu.CompilerParams(dimension_semantics=("arbitrary", "arbitrary"),
                                             vmem_limit_bytes=VMEM_LIMIT),
        name="proj",
    )(h, w["wq"], w["wkt"], w["wv"], w["wconf"], w["wsc"], w["wdw"], w["bdw"], w["lng"],
      w["lnb"], w["wsh"], w["gconf"], w["gsc"], ctail, stail)


def _sb_tile(qh, kt, vt, tri, ones, r_prev, mask):
    z = _dot(qh, kt)
    l1 = jnp.log2(1.0 + jnp.exp2(-jnp.abs(z)))
    log_beta = jnp.minimum(z, 0.0) - l1
    log_keep = log_beta - z
    if mask is not None:
        log_keep = jnp.where(mask, log_keep, 0.0)
    lk = log_keep.astype(BF16)
    between = _dot(lk, tri)
    reps = z.shape[1] // LANES
    log_a = log_beta + between + jnp.concatenate([r_prev] * reps, axis=1)
    a = jnp.exp2(log_a)
    if mask is not None:
        a = jnp.where(mask, a, 0.0)
    return _dot(a.astype(BF16), vt), _dot(lk, ones)


def _attn_kernel(q_ref, kt_ref, v_ref, ktm_ref, vm_ref, tri_ref, ones_ref, o_ref,
                 acc_ref, r_ref, *, tq, n_ctx):
    i = pl.program_id(2)
    q2 = q_ref[0]
    lane = lax.broadcasted_iota(jnp.int32, (tq, LANES), 1)
    zero = jnp.zeros_like(q2)
    qh = (jnp.where(lane < HEAD_DIM, q2, zero), jnp.where(lane >= HEAD_DIM, q2, zero))

    row = lax.broadcasted_iota(jnp.int32, (tq, KV_TILE), 0)
    col = lax.broadcasted_iota(jnp.int32, (tq, KV_TILE), 1)
    diag_mask = col < row
    kt = kt_ref[0, i]
    vt = v_ref[0, pl.ds(pl.multiple_of(i * KV_TILE, KV_TILE), KV_TILE), :]
    r0 = jnp.zeros((tq, LANES), F32)
    for hd in range(2):
        pv, rs = _sb_tile(qh[hd], kt, vt, tri_ref[...], ones_ref[...], r0, diag_mask)
        acc_ref[hd] = pv
        r_ref[hd] = rs

    def body(jj, carry):
        j = i - 1 - jj
        kt_j = kt_ref[0, j]
        vt_j = v_ref[0, pl.ds(pl.multiple_of(j * KV_TILE, KV_TILE), KV_TILE), :]
        for hd in range(2):
            pv, rs = _sb_tile(qh[hd], kt_j, vt_j, tri_ref[...], ones_ref[...], r_ref[hd], None)
            acc_ref[hd] += pv
            r_ref[hd] += rs
        return carry

    lax.fori_loop(0, i, body, 0)

    if n_ctx:
        ctx_mask = lax.broadcasted_iota(jnp.int32, (tq, LANES), 1) < n_ctx
        kt_m = ktm_ref[0, 0][:, :LANES]
        vt_m = vm_ref[0]
        tri_m = tri_ref[0:LANES, 0:LANES]
        ones_m = ones_ref[0:LANES, :]
        for hd in range(2):
            pv, _ = _sb_tile(qh[hd], kt_m, vt_m, tri_m, ones_m, r_ref[hd], ctx_mask)
            acc_ref[hd] += pv

    o_ref[0] = jnp.where(lane < HEAD_DIM, acc_ref[0], acc_ref[1])


def _attn(q, kt, v, ktm, vm, tri, ones, *, n_ctx):
    bsz, length, _ = q.shape
    tq = KV_TILE
    n_kv = length // KV_TILE
    n_pairs = SB_WIDTH // LANES
    return pl.pallas_call(
        functools.partial(_attn_kernel, tq=tq, n_ctx=n_ctx),
        grid=(bsz, n_pairs, length // tq),
        in_specs=[
            pl.BlockSpec((1, tq, LANES), lambda b, p, i: (b, i, p)),
            pl.BlockSpec((1, n_kv, LANES, KV_TILE), lambda b, p, i: (b, 0, p, 0)),
            pl.BlockSpec((1, length, LANES), lambda b, p, i: (b, 0, p)),
            pl.BlockSpec((1, 1, LANES, KV_TILE), lambda b, p, i: (0, 0, p, 0)),
            pl.BlockSpec((1, LANES, LANES), lambda b, p, i: (0, 0, p)),
            _const_spec((KV_TILE, KV_TILE)), _const_spec((KV_TILE, LANES)),
        ],
        out_specs=pl.BlockSpec((1, tq, LANES), lambda b, p, i: (b, i, p)),
        out_shape=jax.ShapeDtypeStruct((bsz, length, SB_WIDTH), F32),
        scratch_shapes=[pltpu.VMEM((2, tq, LANES), F32), pltpu.VMEM((2, tq, LANES), F32)],
        compiler_params=pltpu.CompilerParams(
            dimension_semantics=("arbitrary", "arbitrary", "arbitrary"),
            vmem_limit_bytes=VMEM_LIMIT),
        name="sb_attn",
    )(q, kt, v, ktm, vm, tri, ones)


def _post_kernel(h_ref, o_ref, yc_ref, ys_ref, gsb_ref, wout_ref, lmg_ref, lmb_ref,
                 w1_ref, w2_ref, lfg_ref, lfb_ref, out_ref):
    ysb = _rms_norm(o_ref[0], gsb_ref[...]).astype(BF16)
    y = jnp.concatenate([ysb, yc_ref[0], ys_ref[0]], axis=-1)
    h1 = _layer_norm(ALPHA * h_ref[0] + _dot(y, wout_ref[...]), lmg_ref[...], lmb_ref[...])
    u = jnp.maximum(_dot(h1.astype(BF16), w1_ref[...]), 0.0)
    ff = _dot((u * u).astype(BF16), w2_ref[...])
    out_ref[0] = _layer_norm(ALPHA * h1 + ff, lfg_ref[...], lfb_ref[...])


def _post(h, o, yc, ys, w, *, tm):
    bsz, length, _ = h.shape
    row = lambda width: pl.BlockSpec((1, tm, width), lambda b, i: (b, i, 0))
    return pl.pallas_call(
        _post_kernel,
        grid=(bsz, length // tm),
        in_specs=[row(D_MODEL), row(SB_WIDTH), row(CONF_WIDTH), row(SC_WIDTH),
                  _const_spec((1, SB_WIDTH)), _const_spec((D_MODEL, D_MODEL)),
                  _const_spec((1, D_MODEL)), _const_spec((1, D_MODEL)),
                  _const_spec((D_MODEL, D_FF)), _const_spec((D_FF, D_MODEL)),
                  _const_spec((1, D_MODEL)), _const_spec((1, D_MODEL))],
        out_specs=row(D_MODEL),
        out_shape=jax.ShapeDtypeStruct((bsz, length, D_MODEL), F32),
        compiler_params=pltpu.CompilerParams(dimension_semantics=("arbitrary", "arbitrary"),
                                             vmem_limit_bytes=VMEM_LIMIT),
        name="post",
    )(h, o, yc, ys, w["gsb"], w["wout"], w["lmg"], w["lmb"], w["w1"], w["w2"], w["lfg"], w["lfb"])


def _layer_weights(l, w_in, w_conf_dw, b_conf_dw, ln_conf_g, ln_conf_b, w_short_dw, g_mix, w_out,
                   ln_mix_g, ln_mix_b, w_ff1, w_ff2, ln_ff_g, ln_ff_b):
    wi = w_in[l]
    q_scale = LOG2E * HEAD_DIM ** -0.5
    r = lambda a: a.reshape(1, -1)
    return dict(
        wq=(wi[:, :SB_WIDTH] * q_scale).astype(BF16),
        wkt=wi[:, SB_WIDTH:2 * SB_WIDTH].T.astype(BF16),
        wv=wi[:, 2 * SB_WIDTH:3 * SB_WIDTH].astype(BF16),
        wconf=wi[:, 3 * SB_WIDTH:3 * SB_WIDTH + 2 * CONF_WIDTH].astype(BF16),
        wsc=wi[:, 3 * SB_WIDTH + 2 * CONF_WIDTH:].astype(BF16),
        wdw=w_conf_dw[l], bdw=r(b_conf_dw[l]), lng=r(ln_conf_g[l]), lnb=r(ln_conf_b[l]),
        wsh=w_short_dw[l],
        gsb=r(g_mix[l, :SB_WIDTH]), gconf=r(g_mix[l, SB_WIDTH:SB_WIDTH + CONF_WIDTH]),
        gsc=r(g_mix[l, SB_WIDTH + CONF_WIDTH:]),
        wout=w_out[l].astype(BF16), lmg=r(ln_mix_g[l]), lmb=r(ln_mix_b[l]),
        w1=w_ff1[l].astype(BF16), w2=w_ff2[l].astype(BF16), lfg=r(ln_ff_g[l]), lfb=r(ln_ff_b[l]),
    )


def kernel(x, meta_tokens, ln_in_g, ln_in_b, w_in, w_conf_dw, b_conf_dw, ln_conf_g, ln_conf_b,
           w_short_dw, g_mix, w_out, ln_mix_g, ln_mix_b, w_ff1, w_ff2, ln_ff_g, ln_ff_b):
    bsz, seq, _ = x.shape
    g_in, b_in = ln_in_g.reshape(1, -1), ln_in_b.reshape(1, -1)
    meta = jnp.zeros((META_ROWS, D_MODEL), F32).at[:N_META].set(meta_tokens.astype(F32))
    hm = _ln_rows(meta, g_in, b_in, META_ROWS).reshape(1, META_ROWS, D_MODEL)
    hx = _ln_rows(x.reshape(bsz * seq, D_MODEL), g_in, b_in, 512).reshape(bsz, seq, D_MODEL)

    rows = lax.broadcasted_iota(jnp.int32, (KV_TILE, KV_TILE), 0)
    cols = lax.broadcasted_iota(jnp.int32, (KV_TILE, KV_TILE), 1)
    tri = (rows > cols).astype(BF16)
    ones = jnp.ones((KV_TILE, LANES), BF16)
    zero_ctail = jnp.zeros((1, CONF_HALO, CONF_WIDTH), F32)
    zero_stail = jnp.zeros((1, SC_HALO, SC_WIDTH), F32)

    for l in range(DEPTH):
        w = _layer_weights(l, w_in, w_conf_dw, b_conf_dw, ln_conf_g, ln_conf_b, w_short_dw, g_mix,
                           w_out, ln_mix_g, ln_mix_b, w_ff1, w_ff2, ln_ff_g, ln_ff_b)
        qm, ktm, vm, ycm, ysm, ctail, stail = _proj(hm, w, zero_ctail, zero_stail,
                                                    tm=META_ROWS, tail_at=N_META)
        om = _attn(qm, ktm, vm, ktm, vm, tri, ones, n_ctx=0)
        hm = _post(hm, om, ycm, ysm, w, tm=META_ROWS)

        q, kt, v, yc, ys, _, _ = _proj(hx, w, ctail, stail, tm=512, tail_at=512)
        o = _attn(q, kt, v, ktm, vm, tri, ones, n_ctx=N_META)
        hx = _post(hx, o, yc, ys, w, tm=256)
    return hx
```

```python
import functools
import math

import jax
import jax.numpy as jnp
from jax import lax
from jax.experimental import pallas as pl
from jax.experimental.pallas import tpu as pltpu

D_MODEL = 1024
DEPTH = 4
N_META = 16
SB_WIDTH = 512
HEAD_DIM = 64
CONF_WIDTH = 256
CONF_KERNEL = 31
SC_WIDTH = 256
SC_KERNEL = 3
D_FF = 4 * D_MODEL
ALPHA = (2.0 * DEPTH) ** 0.25
LN_EPS = 1e-5
RMS_EPS = 1e-6
LOG2E = math.log2(math.e)

LANES = 128
KV_TILE = 256
CONF_HALO = 32
SC_HALO = 8
META_ROWS = 256
VMEM_LIMIT = 56 * 1024 * 1024

BF16 = jnp.bfloat16
F32 = jnp.float32


def _dot(a, b):
    return jnp.dot(a, b, preferred_element_type=F32)


def _layer_norm(x, g, b):
    mu = jnp.mean(x, axis=-1, keepdims=True)
    xc = x - mu
    var = jnp.mean(xc * xc, axis=-1, keepdims=True)
    return xc * lax.rsqrt(var + LN_EPS) * g + b


def _rms_norm(x, g):
    return x * lax.rsqrt(jnp.mean(x * x, axis=-1, keepdims=True) + RMS_EPS) * g


def _const_spec(shape):
    zeros = (0,) * len(shape)
    return pl.BlockSpec(shape, lambda *_: zeros)


def _ln_kernel(x_ref, g_ref, b_ref, o_ref):
    o_ref[...] = _layer_norm(x_ref[...], g_ref[...], b_ref[...])


def _ln_rows(x2d, g, b, tm):
    rows = x2d.shape[0]
    return pl.pallas_call(
        _ln_kernel,
        grid=(rows // tm,),
        in_specs=[pl.BlockSpec((tm, D_MODEL), lambda i: (i, 0)),
                  _const_spec((1, D_MODEL)), _const_spec((1, D_MODEL))],
        out_specs=pl.BlockSpec((tm, D_MODEL), lambda i: (i, 0)),
        out_shape=jax.ShapeDtypeStruct((rows, D_MODEL), F32),
        compiler_params=pltpu.CompilerParams(dimension_semantics=("arbitrary",),
                                             vmem_limit_bytes=VMEM_LIMIT),
        name="ln_in",
    )(x2d, g, b)


def _proj_kernel(h_ref, wq_ref, wkt_ref, wv_ref, wconf_ref, wsc_ref, wdw_ref, bdw_ref, lng_ref,
                 lnb_ref, wsh_ref, gconf_ref, gsc_ref, ctail_ref, stail_ref,
                 q_ref, kt_ref, v_ref, yc_ref, ys_ref, ctail_out, stail_out,
                 xc_ref, xs_ref, *, tm, tail_at):
    i = pl.program_id(1)

    @pl.when(i == 0)
    def _():
        xc_ref[0:CONF_HALO, :] = ctail_ref[0]
        xs_ref[0:SC_HALO, :] = stail_ref[0]

    hb = h_ref[0].astype(BF16)
    q_ref[0] = _dot(hb, wq_ref[...]).astype(BF16)
    kt = lax.dot_general(wkt_ref[...], hb, (((1,), (1,)), ((), ())),
                         preferred_element_type=F32)
    for c in range(tm // KV_TILE):
        kt_ref[0, c] = kt[:, c * KV_TILE:(c + 1) * KV_TILE].astype(BF16)
    v_ref[0] = _dot(hb, wv_ref[...]).astype(BF16)

    uc = _dot(hb, wconf_ref[...])
    xc_ref[CONF_HALO:CONF_HALO + tm, :] = uc[:, :CONF_WIDTH] * jax.nn.sigmoid(uc[:, CONF_WIDTH:])
    acc = jnp.broadcast_to(bdw_ref[...], (tm, CONF_WIDTH))
    for k in range(CONF_KERNEL):
        off = CONF_HALO - (CONF_KERNEL - 1) + k
        acc = acc + wdw_ref[k:k + 1, :] * xc_ref[off:off + tm, :]
    y = _layer_norm(acc, lng_ref[...], lnb_ref[...])
    y = y * jax.nn.sigmoid(y)
    yc_ref[0] = _rms_norm(y, gconf_ref[...]).astype(BF16)
    ctail_out[0] = xc_ref[tail_at:tail_at + CONF_HALO, :]
    xc_ref[0:CONF_HALO, :] = xc_ref[tm:tm + CONF_HALO, :]

    us = _dot(hb, wsc_ref[...])
    xs_ref[SC_HALO:SC_HALO + tm, :] = us[:, SC_WIDTH:2 * SC_WIDTH] * us[:, 2 * SC_WIDTH:]
    conv = None
    for k in range(SC_KERNEL):
        off = SC_HALO - (SC_KERNEL - 1) + k
        term = wsh_ref[k:k + 1, :] * xs_ref[off:off + tm, :]
        conv = term if conv is None else conv + term
    ys_ref[0] = _rms_norm(us[:, :SC_WIDTH] * conv, gsc_ref[...]).astype(BF16)
    stail_out[0] = xs_ref[tail_at:tail_at + SC_HALO, :]
    xs_ref[0:SC_HALO, :] = xs_ref[tm:tm + SC_HALO, :]


def _proj(h, w, ctail, stail, *, tm, tail_at):
    bsz, length, _ = h.shape
    n_kv = length // KV_TILE
    row = lambda width: pl.BlockSpec((1, tm, width), lambda b, i: (b, i, 0))
    in_specs = [
        row(D_MODEL),
        _const_spec((D_MODEL, SB_WIDTH)), _const_spec((SB_WIDTH, D_MODEL)),
        _const_spec((D_MODEL, SB_WIDTH)), _const_spec((D_MODEL, 2 * CONF_WIDTH)),
        _const_spec((D_MODEL, 3 * SC_WIDTH)),
        _const_spec((CONF_KERNEL, CONF_WIDTH)), _const_spec((1, CONF_WIDTH)),
        _const_spec((1, CONF_WIDTH)), _const_spec((1, CONF_WIDTH)),
        _const_spec((SC_KERNEL, SC_WIDTH)), _const_spec((1, CONF_WIDTH)), _const_spec((1, SC_WIDTH)),
        _const_spec((1, CONF_HALO, CONF_WIDTH)), _const_spec((1, SC_HALO, SC_WIDTH)),
    ]
    out_specs = [
        row(SB_WIDTH),
        pl.BlockSpec((1, tm // KV_TILE, SB_WIDTH, KV_TILE), lambda b, i: (b, i, 0, 0)),
        row(SB_WIDTH), row(CONF_WIDTH), row(SC_WIDTH),
        pl.BlockSpec((1, CONF_HALO, CONF_WIDTH), lambda b, i: (b, 0, 0)),
        pl.BlockSpec((1, SC_HALO, SC_WIDTH), lambda b, i: (b, 0, 0)),
    ]
    out_shape = [
        jax.ShapeDtypeStruct((bsz, length, SB_WIDTH), BF16),
        jax.ShapeDtypeStruct((bsz, n_kv, SB_WIDTH, KV_TILE), BF16),
        jax.ShapeDtypeStruct((bsz, length, SB_WIDTH), BF16),
        jax.ShapeDtypeStruct((bsz, length, CONF_WIDTH), BF16),
        jax.ShapeDtypeStruct((bsz, length, SC_WIDTH), BF16),
        jax.ShapeDtypeStruct((bsz, CONF_HALO, CONF_WIDTH), F32),
        jax.ShapeDtypeStruct((bsz, SC_HALO, SC_WIDTH), F32),
    ]
    return pl.pallas_call(
        functools.partial(_proj_kernel, tm=tm, tail_at=tail_at),
        grid=(bsz, length // tm),
        in_specs=in_specs, out_specs=out_specs, out_shape=out_shape,
        scratch_shapes=[pltpu.VMEM((tm + CONF_HALO, CONF_WIDTH), F32),
                        pltpu.VMEM((tm + SC_HALO, SC_WIDTH), F32)],
        compiler_params=pltpu.CompilerParams(dimension_semantics=("arbitrary", "arbitrary"),
                                             vmem_limit_bytes=VMEM_LIMIT),
        name="proj",
    )(h, w["wq"], w["wkt"], w["wv"], w["wconf"], w["wsc"], w["wdw"], w["bdw"], w["lng"],
      w["lnb"], w["wsh"], w["gconf"], w["gsc"], ctail, stail)


def _sb_key_tile(qh_ref, kts, vts, tri, ones, mask, first, update_r,
                 lk_ref, lb_ref, a_ref, acc_ref, r_ref):
    n_heads = len(kts) * 2
    width = kts[0].shape[1]
    reps = width // LANES
    zs = {}
    for step in range(n_heads + 2):
        if step < n_heads:
            zs[step] = _dot(qh_ref[step], kts[step // 2])
        h = step - 1
        if 0 <= h < n_heads:
            z = zs.pop(h)
            log_beta = jnp.minimum(z, 0.0) - jnp.log2(1.0 + jnp.exp2(-jnp.abs(z)))
            log_keep = log_beta - z
            if mask is not None:
                log_keep = jnp.where(mask, log_keep, 0.0)
            lb_ref[h, :, 0:width] = log_beta
            lk_ref[h, :, 0:width] = log_keep.astype(BF16)
        h = step - 2
        if 0 <= h < n_heads:
            lk = lk_ref[h, :, 0:width]
            log_a = lb_ref[h, :, 0:width] + _dot(lk, tri)
            if not first:
                log_a = log_a + jnp.concatenate([r_ref[h]] * reps, axis=1)
            a = jnp.exp2(log_a)
            if mask is not None:
                a = jnp.where(mask, a, 0.0)
            a_ref[h, :, 0:width] = a.astype(BF16)
            pv = _dot(a_ref[h, :, 0:width], vts[h // 2])
            acc_ref[h] = pv if first else acc_ref[h] + pv
            if update_r:
                rs = _dot(lk, ones)
                r_ref[h] = rs if first else r_ref[h] + rs


def _attn_kernel(q_ref, kt_ref, v_ref, ktm_ref, vm_ref, tri_ref, ones_ref, o_ref,
                 qh_ref, lk_ref, lb_ref, a_ref, acc_ref, r_ref, *, tq, n_ctx):
    i = pl.program_id(1)
    n_pairs = SB_WIDTH // LANES
    lane = lax.broadcasted_iota(jnp.int32, (tq, LANES), 1)
    for p in range(n_pairs):
        q2 = q_ref[0, :, p * LANES:(p + 1) * LANES]
        zero = jnp.zeros_like(q2)
        qh_ref[2 * p] = jnp.where(lane < HEAD_DIM, q2, zero)
        qh_ref[2 * p + 1] = jnp.where(lane >= HEAD_DIM, q2, zero)
    scratch = (lk_ref, lb_ref, a_ref, acc_ref, r_ref)

    def key_tile(j):
        start = pl.multiple_of(j * KV_TILE, KV_TILE)
        return ([kt_ref[0, j, p * LANES:(p + 1) * LANES, :] for p in range(n_pairs)],
                [v_ref[0, pl.ds(start, KV_TILE), p * LANES:(p + 1) * LANES] for p in range(n_pairs)])

    row = lax.broadcasted_iota(jnp.int32, (tq, KV_TILE), 0)
    col = lax.broadcasted_iota(jnp.int32, (tq, KV_TILE), 1)
    kts, vts = key_tile(i)
    _sb_key_tile(qh_ref, kts, vts, tri_ref[...], ones_ref[...], col < row, True, True, *scratch)

    def body(jj, carry):
        kts_j, vts_j = key_tile(i - 1 - jj)
        _sb_key_tile(qh_ref, kts_j, vts_j, tri_ref[...], ones_ref[...], None, False, True, *scratch)
        return carry

    lax.fori_loop(0, i, body, 0)

    if n_ctx:
        kts_m = [ktm_ref[0, 0, p * LANES:(p + 1) * LANES, 0:LANES] for p in range(n_pairs)]
        vts_m = [vm_ref[0, :, p * LANES:(p + 1) * LANES] for p in range(n_pairs)]
        _sb_key_tile(qh_ref, kts_m, vts_m, tri_ref[0:LANES, 0:LANES], None, lane < n_ctx,
                     False, False, *scratch)

    for p in range(n_pairs):
        o_ref[0, :, p * LANES:(p + 1) * LANES] = jnp.where(lane < HEAD_DIM, acc_ref[2 * p],
                                                           acc_ref[2 * p + 1])


def _attn(q, kt, v, ktm, vm, tri, ones, *, n_ctx):
    bsz, length, _ = q.shape
    tq = KV_TILE
    n_kv = length // KV_TILE
    n_heads = SB_WIDTH // HEAD_DIM
    return pl.pallas_call(
        functools.partial(_attn_kernel, tq=tq, n_ctx=n_ctx),
        grid=(bsz, length // tq),
        in_specs=[
            pl.BlockSpec((1, tq, SB_WIDTH), lambda b, i: (b, i, 0)),
            pl.BlockSpec((1, n_kv, SB_WIDTH, KV_TILE), lambda b, i: (b, 0, 0, 0)),
            pl.BlockSpec((1, length, SB_WIDTH), lambda b, i: (b, 0, 0)),
            pl.BlockSpec((1, 1, SB_WIDTH, KV_TILE), lambda b, i: (0, 0, 0, 0)),
            pl.BlockSpec((1, LANES, SB_WIDTH), lambda b, i: (0, 0, 0)),
            _const_spec((KV_TILE, KV_TILE)), _const_spec((KV_TILE, LANES)),
        ],
        out_specs=pl.BlockSpec((1, tq, SB_WIDTH), lambda b, i: (b, i, 0)),
        out_shape=jax.ShapeDtypeStruct((bsz, length, SB_WIDTH), F32),
        scratch_shapes=[pltpu.VMEM((n_heads, tq, LANES), BF16),
                        pltpu.VMEM((n_heads, tq, KV_TILE), BF16),
                        pltpu.VMEM((n_heads, tq, KV_TILE), F32),
                        pltpu.VMEM((n_heads, tq, KV_TILE), BF16),
                        pltpu.VMEM((n_heads, tq, LANES), F32),
                        pltpu.VMEM((n_heads, tq, LANES), F32)],
        compiler_params=pltpu.CompilerParams(dimension_semantics=("arbitrary", "arbitrary"),
                                             vmem_limit_bytes=VMEM_LIMIT),
        name="sb_attn",
    )(q, kt, v, ktm, vm, tri, ones)


def _post_kernel(h_ref, o_ref, yc_ref, ys_ref, gsb_ref, wout_ref, lmg_ref, lmb_ref,
                 w1_ref, w2_ref, lfg_ref, lfb_ref, out_ref):
    ysb = _rms_norm(o_ref[0], gsb_ref[...]).astype(BF16)
    y = jnp.concatenate([ysb, yc_ref[0], ys_ref[0]], axis=-1)
    h1 = _layer_norm(ALPHA * h_ref[0] + _dot(y, wout_ref[...]), lmg_ref[...], lmb_ref[...])
    u = jnp.maximum(_dot(h1.astype(BF16), w1_ref[...]), 0.0)
    ff = _dot((u * u).astype(BF16), w2_ref[...])
    out_ref[0] = _layer_norm(ALPHA * h1 + ff, lfg_ref[...], lfb_ref[...])


def _post(h, o, yc, ys, w, *, tm):
    bsz, length, _ = h.shape
    row = lambda width: pl.BlockSpec((1, tm, width), lambda b, i: (b, i, 0))
    return pl.pallas_call(
        _post_kernel,
        grid=(bsz, length // tm),
        in_specs=[row(D_MODEL), row(SB_WIDTH), row(CONF_WIDTH), row(SC_WIDTH),
                  _const_spec((1, SB_WIDTH)), _const_spec((D_MODEL, D_MODEL)),
                  _const_spec((1, D_MODEL)), _const_spec((1, D_MODEL)),
                  _const_spec((D_MODEL, D_FF)), _const_spec((D_FF, D_MODEL)),
                  _const_spec((1, D_MODEL)), _const_spec((1, D_MODEL))],
        out_specs=row(D_MODEL),
        out_shape=jax.ShapeDtypeStruct((bsz, length, D_MODEL), F32),
        compiler_params=pltpu.CompilerParams(dimension_semantics=("arbitrary", "arbitrary"),
                                             vmem_limit_bytes=VMEM_LIMIT),
        name="post",
    )(h, o, yc, ys, w["gsb"], w["wout"], w["lmg"], w["lmb"], w["w1"], w["w2"], w["lfg"], w["lfb"])


def _layer_weights(l, w_in, w_conf_dw, b_conf_dw, ln_conf_g, ln_conf_b, w_short_dw, g_mix, w_out,
                   ln_mix_g, ln_mix_b, w_ff1, w_ff2, ln_ff_g, ln_ff_b):
    wi = w_in[l]
    q_scale = LOG2E * HEAD_DIM ** -0.5
    r = lambda a: a.reshape(1, -1)
    return dict(
        wq=(wi[:, :SB_WIDTH] * q_scale).astype(BF16),
        wkt=wi[:, SB_WIDTH:2 * SB_WIDTH].T.astype(BF16),
        wv=wi[:, 2 * SB_WIDTH:3 * SB_WIDTH].astype(BF16),
        wconf=wi[:, 3 * SB_WIDTH:3 * SB_WIDTH + 2 * CONF_WIDTH].astype(BF16),
        wsc=wi[:, 3 * SB_WIDTH + 2 * CONF_WIDTH:].astype(BF16),
        wdw=w_conf_dw[l], bdw=r(b_conf_dw[l]), lng=r(ln_conf_g[l]), lnb=r(ln_conf_b[l]),
        wsh=w_short_dw[l],
        gsb=r(g_mix[l, :SB_WIDTH]), gconf=r(g_mix[l, SB_WIDTH:SB_WIDTH + CONF_WIDTH]),
        gsc=r(g_mix[l, SB_WIDTH + CONF_WIDTH:]),
        wout=w_out[l].astype(BF16), lmg=r(ln_mix_g[l]), lmb=r(ln_mix_b[l]),
        w1=w_ff1[l].astype(BF16), w2=w_ff2[l].astype(BF16), lfg=r(ln_ff_g[l]), lfb=r(ln_ff_b[l]),
    )


def kernel(x, meta_tokens, ln_in_g, ln_in_b, w_in, w_conf_dw, b_conf_dw, ln_conf_g, ln_conf_b,
           w_short_dw, g_mix, w_out, ln_mix_g, ln_mix_b, w_ff1, w_ff2, ln_ff_g, ln_ff_b):
    bsz, seq, _ = x.shape
    g_in, b_in = ln_in_g.reshape(1, -1), ln_in_b.reshape(1, -1)
    meta = jnp.zeros((META_ROWS, D_MODEL), F32).at[:N_META].set(meta_tokens.astype(F32))
    hm = _ln_rows(meta, g_in, b_in, META_ROWS).reshape(1, META_ROWS, D_MODEL)
    hx = _ln_rows(x.reshape(bsz * seq, D_MODEL), g_in, b_in, 512).reshape(bsz, seq, D_MODEL)

    rows = lax.broadcasted_iota(jnp.int32, (KV_TILE, KV_TILE), 0)
    cols = lax.broadcasted_iota(jnp.int32, (KV_TILE, KV_TILE), 1)
    tri = (rows > cols).astype(BF16)
    ones = jnp.ones((KV_TILE, LANES), BF16)
    zero_ctail = jnp.zeros((1, CONF_HALO, CONF_WIDTH), F32)
    zero_stail = jnp.zeros((1, SC_HALO, SC_WIDTH), F32)

    for l in range(DEPTH):
        w = _layer_weights(l, w_in, w_conf_dw, b_conf_dw, ln_conf_g, ln_conf_b, w_short_dw, g_mix,
                           w_out, ln_mix_g, ln_mix_b, w_ff1, w_ff2, ln_ff_g, ln_ff_b)
        qm, ktm, vm, ycm, ysm, ctail, stail = _proj(hm, w, zero_ctail, zero_stail,
                                                    tm=META_ROWS, tail_at=N_META)
        om = _attn(qm, ktm, vm, ktm, vm, tri, ones, n_ctx=0)
        hm = _post(hm, om, ycm, ysm, w, tm=META_ROWS)

        q, kt, v, yc, ys, _, _ = _proj(hx, w, ctail, stail, tm=512, tail_at=512)
        o = _attn(q, kt, v, ktm, vm, tri, ones, n_ctx=N_META)
        hx = _post(hx, o, yc, ys, w, tm=256)
    return hx
```

```python
import functools
import math

import jax
import jax.numpy as jnp
from jax import lax
from jax.experimental import pallas as pl
from jax.experimental.pallas import tpu as pltpu

D_MODEL = 1024
DEPTH = 4
N_META = 16
SB_WIDTH = 512
HEAD_DIM = 64
CONF_WIDTH = 256
CONF_KERNEL = 31
SC_WIDTH = 256
SC_KERNEL = 3
D_FF = 4 * D_MODEL
ALPHA = (2.0 * DEPTH) ** 0.25
LN_EPS = 1e-5
RMS_EPS = 1e-6
LOG2E = math.log2(math.e)

LANES = 128
SUBLANES = 8
CONV_CHUNK = 64
SOFTPLUS2_CAP = 64.0
KV_TILE = 256
CONF_HALO = 32
SC_HALO = 8
META_ROWS = 256
VMEM_LIMIT = 56 * 1024 * 1024

BF16 = jnp.bfloat16
F32 = jnp.float32


def _dot(a, b):
    return jnp.dot(a, b, preferred_element_type=F32)


def _layer_norm(x, g, b):
    mu = jnp.mean(x, axis=-1, keepdims=True)
    xc = x - mu
    var = jnp.mean(xc * xc, axis=-1, keepdims=True)
    return xc * lax.rsqrt(var + LN_EPS) * g + b


def _rms_norm(x, g):
    return x * lax.rsqrt(jnp.mean(x * x, axis=-1, keepdims=True) + RMS_EPS) * g


def _const_spec(shape):
    zeros = (0,) * len(shape)
    return pl.BlockSpec(shape, lambda *_: zeros)


def _ln_kernel(x_ref, g_ref, b_ref, o_ref):
    o_ref[...] = _layer_norm(x_ref[...], g_ref[...], b_ref[...])


def _ln_rows(x2d, g, b, tm):
    rows = x2d.shape[0]
    return pl.pallas_call(
        _ln_kernel,
        grid=(rows // tm,),
        in_specs=[pl.BlockSpec((tm, D_MODEL), lambda i: (i, 0)),
                  _const_spec((1, D_MODEL)), _const_spec((1, D_MODEL))],
        out_specs=pl.BlockSpec((tm, D_MODEL), lambda i: (i, 0)),
        out_shape=jax.ShapeDtypeStruct((rows, D_MODEL), F32),
        compiler_params=pltpu.CompilerParams(dimension_semantics=("arbitrary",),
                                             vmem_limit_bytes=VMEM_LIMIT),
        name="ln_in",
    )(x2d, g, b)


def _proj_kernel(h_ref, wq_ref, wkt_ref, wv_ref, wconf_ref, wsc_ref, wdw_ref, bdw_ref, lng_ref,
                 lnb_ref, wsh_ref, gconf_ref, gsc_ref, ctail_ref, stail_ref,
                 q_ref, kt_ref, v_ref, yc_ref, ys_ref, ctail_out, stail_out,
                 xc_ref, xs_ref, zs_ref, bg_ref, *, tm, tail_at):
    i = pl.program_id(1)

    @pl.when(i == 0)
    def _():
        xc_ref[0:CONF_HALO, :] = ctail_ref[0]
        xs_ref[0:SC_HALO, :] = stail_ref[0]

    hb = h_ref[0].astype(BF16)
    cw, sw = CONF_WIDTH, SC_WIDTH

    xc_ref[CONF_HALO:CONF_HALO + tm, :] = (_dot(hb, wconf_ref[:, 0:cw])
                                           * jax.nn.sigmoid(_dot(hb, wconf_ref[:, cw:2 * cw])))
    ctail_out[0] = xc_ref[tail_at:tail_at + CONF_HALO, :]
    z_rows = tm + CONF_HALO - SUBLANES
    for r in range(SUBLANES):
        zs_ref[r] = xc_ref[SUBLANES - r:SUBLANES - r + z_rows, :]
    bg_ref[...] = _dot(hb, wsc_ref[:, 0:sw])
    xs_ref[SC_HALO:SC_HALO + tm, :] = (_dot(hb, wsc_ref[:, sw:2 * sw])
                                       * _dot(hb, wsc_ref[:, 2 * sw:3 * sw]))
    stail_out[0] = xs_ref[tail_at:tail_at + SC_HALO, :]

    def conf_chunk(t0, rows):
        acc = jnp.broadcast_to(bdw_ref[...], (rows, cw))
        for d in range(CONF_KERNEL):
            a, r = divmod(d, SUBLANES)
            off = t0 + CONF_HALO - SUBLANES - SUBLANES * a
            k = CONF_KERNEL - 1 - d
            acc = acc + wdw_ref[k:k + 1, :] * zs_ref[r, off:off + rows, :]
        y = _layer_norm(acc, lng_ref[...], lnb_ref[...])
        y = y * jax.nn.sigmoid(y)
        yc_ref[0, t0:t0 + rows, :] = _rms_norm(y, gconf_ref[...]).astype(BF16)

    def short_chunk(t0, rows):
        conv = None
        for k in range(SC_KERNEL):
            off = t0 + SC_HALO - (SC_KERNEL - 1) + k
            term = wsh_ref[k:k + 1, :] * xs_ref[off:off + rows, :]
            conv = term if conv is None else conv + term
        ys_ref[0, t0:t0 + rows, :] = _rms_norm(bg_ref[t0:t0 + rows, :] * conv,
                                               gsc_ref[...]).astype(BF16)

    def q_block(c):
        cols = slice(c * KV_TILE, (c + 1) * KV_TILE)
        q_ref[0, :, cols] = _dot(hb, wq_ref[:, cols]).astype(BF16)

    def v_block(c):
        cols = slice(c * KV_TILE, (c + 1) * KV_TILE)
        v_ref[0, :, cols] = _dot(hb, wv_ref[:, cols]).astype(BF16)

    def kt_block(c):
        rows = slice(c * KV_TILE, (c + 1) * KV_TILE)
        kt_ref[0, c] = lax.dot_general(wkt_ref[...], hb[rows], (((1,), (1,)), ((), ())),
                                       preferred_element_type=F32).astype(BF16)

    matmuls = ([functools.partial(q_block, c) for c in range(SB_WIDTH // KV_TILE)]
               + [functools.partial(kt_block, c) for c in range(tm // KV_TILE)]
               + [functools.partial(v_block, c) for c in range(SB_WIDTH // KV_TILE)])
    vpu_work = ([functools.partial(conf_chunk, t0, CONV_CHUNK) for t0 in range(0, tm, CONV_CHUNK)]
                + [functools.partial(short_chunk, t0, 2 * CONV_CHUNK)
                   for t0 in range(0, tm, 2 * CONV_CHUNK)])
    n_m, n_v = len(matmuls), len(vpu_work)
    for k in range(n_m):
        matmuls[k]()
        for job in vpu_work[k * n_v // n_m:(k + 1) * n_v // n_m]:
            job()

    xc_ref[0:CONF_HALO, :] = xc_ref[tm:tm + CONF_HALO, :]
    xs_ref[0:SC_HALO, :] = xs_ref[tm:tm + SC_HALO, :]


def _proj(h, w, ctail, stail, *, tm, tail_at):
    bsz, length, _ = h.shape
    n_kv = length // KV_TILE
    row = lambda width: pl.BlockSpec((1, tm, width), lambda b, i: (b, i, 0))
    in_specs = [
        row(D_MODEL),
        _const_spec((D_MODEL, SB_WIDTH)), _const_spec((SB_WIDTH, D_MODEL)),
        _const_spec((D_MODEL, SB_WIDTH)), _const_spec((D_MODEL, 2 * CONF_WIDTH)),
        _const_spec((D_MODEL, 3 * SC_WIDTH)),
        _const_spec((CONF_KERNEL, CONF_WIDTH)), _const_spec((1, CONF_WIDTH)),
        _const_spec((1, CONF_WIDTH)), _const_spec((1, CONF_WIDTH)),
        _const_spec((SC_KERNEL, SC_WIDTH)), _const_spec((1, CONF_WIDTH)), _const_spec((1, SC_WIDTH)),
        _const_spec((1, CONF_HALO, CONF_WIDTH)), _const_spec((1, SC_HALO, SC_WIDTH)),
    ]
    out_specs = [
        row(SB_WIDTH),
        pl.BlockSpec((1, tm // KV_TILE, SB_WIDTH, KV_TILE), lambda b, i: (b, i, 0, 0)),
        row(SB_WIDTH), row(CONF_WIDTH), row(SC_WIDTH),
        pl.BlockSpec((1, CONF_HALO, CONF_WIDTH), lambda b, i: (b, 0, 0)),
        pl.BlockSpec((1, SC_HALO, SC_WIDTH), lambda b, i: (b, 0, 0)),
    ]
    out_shape = [
        jax.ShapeDtypeStruct((bsz, length, SB_WIDTH), BF16),
        jax.ShapeDtypeStruct((bsz, n_kv, SB_WIDTH, KV_TILE), BF16),
        jax.ShapeDtypeStruct((bsz, length, SB_WIDTH), BF16),
        jax.ShapeDtypeStruct((bsz, length, CONF_WIDTH), BF16),
        jax.ShapeDtypeStruct((bsz, length, SC_WIDTH), BF16),
        jax.ShapeDtypeStruct((bsz, CONF_HALO, CONF_WIDTH), F32),
        jax.ShapeDtypeStruct((bsz, SC_HALO, SC_WIDTH), F32),
    ]
    return pl.pallas_call(
        functools.partial(_proj_kernel, tm=tm, tail_at=tail_at),
        grid=(bsz, length // tm),
        in_specs=in_specs, out_specs=out_specs, out_shape=out_shape,
        scratch_shapes=[pltpu.VMEM((tm + CONF_HALO, CONF_WIDTH), F32),
                        pltpu.VMEM((tm + SC_HALO, SC_WIDTH), F32),
                        pltpu.VMEM((SUBLANES, tm + CONF_HALO - SUBLANES, CONF_WIDTH), F32),
                        pltpu.VMEM((tm, SC_WIDTH), F32)],
        compiler_params=pltpu.CompilerParams(dimension_semantics=("arbitrary", "arbitrary"),
                                             vmem_limit_bytes=VMEM_LIMIT),
        name="proj",
    )(h, w["wq"], w["wkt"], w["wv"], w["wconf"], w["wsc"], w["wdw"], w["bdw"], w["lng"],
      w["lnb"], w["wsh"], w["gconf"], w["gsc"], ctail, stail)


def _softplus2(z):
    return jnp.maximum(z, jnp.log2(1.0 + jnp.exp2(jnp.minimum(z, SOFTPLUS2_CAP))))


def _sb_section(chains, qh_ref, ntri, mask, first, update_r, z_ref, lk_ref, a_ref, acc_ref, r_ref):
    n = len(chains)
    zs, betweens, pvs = {}, {}, {}
    for step in range(n + 3):
        if step < n:
            c, load_kt, _ = chains[step]
            zs[step] = _dot(qh_ref[c], load_kt())
        if 0 <= step - 1 < n:
            c = chains[step - 1][0]
            z = zs.pop(step - 1)
            width = z.shape[1]
            sp = _softplus2(z)
            if mask is not None:
                sp = jnp.where(mask, sp, 0.0)
            z_ref[c, :, 0:width] = z
            lk_ref[c, :, 0:width] = sp.astype(BF16)
            betweens[step - 1] = _dot(lk_ref[c, :, 0:width], ntri)
        if 0 <= step - 2 < n:
            c, _, load_vt = chains[step - 2]
            between = betweens.pop(step - 2)
            log_a = z_ref[c, :, 0:width] + between
            if not first:
                log_a = log_a + jnp.concatenate([r_ref[c]] * (width // LANES), axis=1)
            a = jnp.exp2(log_a)
            if mask is not None:
                a = jnp.where(mask, a, 0.0)
            a_ref[c, :, 0:width] = a.astype(BF16)
            pvs[step - 2] = _dot(a_ref[c, :, 0:width], load_vt())
            if update_r:
                total = jnp.broadcast_to(between[:, 0:1], (between.shape[0], LANES))
                r_ref[c] = total if first else r_ref[c] + total
        if 0 <= step - 3 < n:
            c = chains[step - 3][0]
            pv = pvs.pop(step - 3)
            acc_ref[c] = pv if first else acc_ref[c] + pv


def _attn_kernel(q_ref, kt_ref, v_ref, kctx_ref, vctx_ref, ntri_ctx_ref, ntri_ref, o_ref,
                 qh_ref, z_ref, lk_ref, a_ref, acc_ref, r_ref, *, n_sub, n_ctx):
    i = pl.program_id(1)
    n_pairs = SB_WIDTH // LANES
    n_heads = 2 * n_pairs
    lane = lax.broadcasted_iota(jnp.int32, (KV_TILE, LANES), 1)
    for r in range(n_sub):
        for p in range(n_pairs):
            q2 = q_ref[0, r * KV_TILE:(r + 1) * KV_TILE, p * LANES:(p + 1) * LANES]
            zero = jnp.zeros_like(q2)
            qh_ref[r * n_heads + 2 * p] = jnp.where(lane < HEAD_DIM, q2, zero)
            qh_ref[r * n_heads + 2 * p + 1] = jnp.where(lane >= HEAD_DIM, q2, zero)
    scratch = (z_ref, lk_ref, a_ref, acc_ref, r_ref)

    def chains_for(rows, tile_of_row):
        out = []
        for r in rows:
            j = tile_of_row(r)
            start = pl.multiple_of(j * KV_TILE, KV_TILE)
            for h in range(n_heads):
                lanes = slice((h // 2) * LANES, (h // 2 + 1) * LANES)
                out.append((r * n_heads + h,
                            functools.partial(lambda j, lanes: kt_ref[0, j, lanes, :], j, lanes),
                            functools.partial(lambda start, lanes: v_ref[0, pl.ds(start, KV_TILE), lanes],
                                              start, lanes)))
        return out

    row = lax.broadcasted_iota(jnp.int32, (KV_TILE, KV_TILE), 0)
    col = lax.broadcasted_iota(jnp.int32, (KV_TILE, KV_TILE), 1)
    all_rows = range(n_sub)
    _sb_section(chains_for(all_rows, lambda r: n_sub * i + r), qh_ref, ntri_ref[...], col < row,
                True, True, *scratch)
    for d in range(1, n_sub):
        _sb_section(chains_for(range(d, n_sub), lambda r: n_sub * i + r - d), qh_ref, ntri_ref[...],
                    None, False, True, *scratch)

    def body(jj, carry):
        _sb_section(chains_for(all_rows, lambda r: n_sub * i - 1 - jj), qh_ref, ntri_ref[...],
                    None, False, True, *scratch)
        return carry

    lax.fori_loop(0, n_sub * i, body, 0)

    for r in all_rows:
        rows_r = slice(r * KV_TILE, (r + 1) * KV_TILE)
        o = jnp.concatenate([jnp.where(lane < HEAD_DIM, acc_ref[r * n_heads + 2 * p],
                                       acc_ref[r * n_heads + 2 * p + 1]) for p in range(n_pairs)], axis=1)
        if n_ctx:
            z = _dot(q_ref[0, rows_r, :], kctx_ref[...])
            sp = _softplus2(z)
            log_a = z + _dot(sp.astype(BF16), ntri_ctx_ref[...])
            for h in range(n_heads):
                log_a = log_a + jnp.where(lane // n_ctx == h, r_ref[r * n_heads + h], 0.0)
            o = o + _dot(jnp.exp2(log_a).astype(BF16), vctx_ref[...])
        o_ref[0, rows_r, :] = o


def _ctx_operands(ktm, vm):
    n_heads = SB_WIDTH // HEAD_DIM
    assert n_heads * N_META == LANES
    kt_ctx = ktm[0, 0, :, :N_META]
    v_ctx = vm[0, :N_META, :]
    head_of_dim = lax.broadcasted_iota(jnp.int32, (SB_WIDTH, LANES), 0) // HEAD_DIM
    head_of_col = lax.broadcasted_iota(jnp.int32, (SB_WIDTH, LANES), 1) // N_META
    same = head_of_dim == head_of_col
    kctx = jnp.where(same, jnp.tile(kt_ctx, (1, n_heads)), 0).astype(BF16)
    vctx = jnp.where(same.T, jnp.tile(v_ctx, (n_heads, 1)), 0).astype(BF16)
    j = lax.broadcasted_iota(jnp.int32, (LANES, LANES), 0)
    s = lax.broadcasted_iota(jnp.int32, (LANES, LANES), 1)
    ntri_ctx = -((j >= s) & (j // N_META == s // N_META)).astype(BF16)
    return kctx, vctx, ntri_ctx


def _attn(q, kt, v, ctx, ntri, *, n_sub, n_ctx):
    bsz, length, _ = q.shape
    tq = n_sub * KV_TILE
    n_kv = length // KV_TILE
    n_chains = n_sub * SB_WIDTH // HEAD_DIM
    return pl.pallas_call(
        functools.partial(_attn_kernel, n_sub=n_sub, n_ctx=n_ctx),
        grid=(bsz, length // tq),
        in_specs=[
            pl.BlockSpec((1, tq, SB_WIDTH), lambda b, i: (b, i, 0)),
            pl.BlockSpec((1, n_kv, SB_WIDTH, KV_TILE), lambda b, i: (b, 0, 0, 0)),
            pl.BlockSpec((1, length, SB_WIDTH), lambda b, i: (b, 0, 0)),
            _const_spec((SB_WIDTH, LANES)), _const_spec((LANES, SB_WIDTH)),
            _const_spec((LANES, LANES)), _const_spec((KV_TILE, KV_TILE)),
        ],
        out_specs=pl.BlockSpec((1, tq, SB_WIDTH), lambda b, i: (b, i, 0)),
        out_shape=jax.ShapeDtypeStruct((bsz, length, SB_WIDTH), F32),
        scratch_shapes=[pltpu.VMEM((n_chains, KV_TILE, LANES), BF16),
                        pltpu.VMEM((n_chains, KV_TILE, KV_TILE), F32),
                        pltpu.VMEM((n_chains, KV_TILE, KV_TILE), BF16),
                        pltpu.VMEM((n_chains, KV_TILE, KV_TILE), BF16),
                        pltpu.VMEM((n_chains, KV_TILE, LANES), F32),
                        pltpu.VMEM((n_chains, KV_TILE, LANES), F32)],
        compiler_params=pltpu.CompilerParams(dimension_semantics=("arbitrary", "arbitrary"),
                                             vmem_limit_bytes=VMEM_LIMIT),
        name="sb_attn",
    )(q, kt, v, *ctx, ntri)


def _post_kernel(h_ref, o_ref, yc_ref, ys_ref, gsb_ref, wout_ref, lmg_ref, lmb_ref,
                 w1_ref, w2_ref, lfg_ref, lfb_ref, out_ref, *, n_sub):
    rows = h_ref.shape[1] // n_sub
    subs = [slice(s * rows, (s + 1) * rows) for s in range(n_sub)]

    def mix(r):
        ysb = _rms_norm(o_ref[0, r, :], gsb_ref[...]).astype(BF16)
        y = jnp.concatenate([ysb, yc_ref[0, r, :], ys_ref[0, r, :]], axis=-1)
        return _layer_norm(ALPHA * h_ref[0, r, :] + _dot(y, wout_ref[...]), lmg_ref[...], lmb_ref[...])

    def ff_in(h1):
        u = jnp.maximum(_dot(h1.astype(BF16), w1_ref[...]), 0.0)
        return (u * u).astype(BF16)

    h1s = [mix(r) for r in subs]
    u2s = [ff_in(h1) for h1 in h1s]
    for r, h1, u2 in zip(subs, h1s, u2s):
        out_ref[0, r, :] = _layer_norm(ALPHA * h1 + _dot(u2, w2_ref[...]), lfg_ref[...], lfb_ref[...])


def _post(h, o, yc, ys, w, *, tm, n_sub):
    bsz, length, _ = h.shape
    row = lambda width: pl.BlockSpec((1, tm, width), lambda b, i: (b, i, 0))
    single = lambda shape: pl.BlockSpec(shape, lambda *_: (0,) * len(shape),
                                        pipeline_mode=pl.Buffered(1))
    return pl.pallas_call(
        functools.partial(_post_kernel, n_sub=n_sub),
        grid=(bsz, length // tm),
        in_specs=[row(D_MODEL), row(SB_WIDTH), row(CONF_WIDTH), row(SC_WIDTH),
                  _const_spec((1, SB_WIDTH)), single((D_MODEL, D_MODEL)),
                  _const_spec((1, D_MODEL)), _const_spec((1, D_MODEL)),
                  single((D_MODEL, D_FF)), single((D_FF, D_MODEL)),
                  _const_spec((1, D_MODEL)), _const_spec((1, D_MODEL))],
        out_specs=row(D_MODEL),
        out_shape=jax.ShapeDtypeStruct((bsz, length, D_MODEL), F32),
        compiler_params=pltpu.CompilerParams(dimension_semantics=("arbitrary", "arbitrary"),
                                             vmem_limit_bytes=VMEM_LIMIT),
        name="post",
    )(h, o, yc, ys, w["gsb"], w["wout"], w["lmg"], w["lmb"], w["w1"], w["w2"], w["lfg"], w["lfb"])


def _layer_weights(l, w_in, w_conf_dw, b_conf_dw, ln_conf_g, ln_conf_b, w_short_dw, g_mix, w_out,
                   ln_mix_g, ln_mix_b, w_ff1, w_ff2, ln_ff_g, ln_ff_b):
    wi = w_in[l]
    q_scale = LOG2E * HEAD_DIM ** -0.5
    r = lambda a: a.reshape(1, -1)
    return dict(
        wq=(wi[:, :SB_WIDTH] * q_scale).astype(BF16),
        wkt=wi[:, SB_WIDTH:2 * SB_WIDTH].T.astype(BF16),
        wv=wi[:, 2 * SB_WIDTH:3 * SB_WIDTH].astype(BF16),
        wconf=wi[:, 3 * SB_WIDTH:3 * SB_WIDTH + 2 * CONF_WIDTH].astype(BF16),
        wsc=wi[:, 3 * SB_WIDTH + 2 * CONF_WIDTH:].astype(BF16),
        wdw=w_conf_dw[l], bdw=r(b_conf_dw[l]), lng=r(ln_conf_g[l]), lnb=r(ln_conf_b[l]),
        wsh=w_short_dw[l],
        gsb=r(g_mix[l, :SB_WIDTH]), gconf=r(g_mix[l, SB_WIDTH:SB_WIDTH + CONF_WIDTH]),
        gsc=r(g_mix[l, SB_WIDTH + CONF_WIDTH:]),
        wout=w_out[l].astype(BF16), lmg=r(ln_mix_g[l]), lmb=r(ln_mix_b[l]),
        w1=w_ff1[l].astype(BF16), w2=w_ff2[l].astype(BF16), lfg=r(ln_ff_g[l]), lfb=r(ln_ff_b[l]),
    )


def kernel(x, meta_tokens, ln_in_g, ln_in_b, w_in, w_conf_dw, b_conf_dw, ln_conf_g, ln_conf_b,
           w_short_dw, g_mix, w_out, ln_mix_g, ln_mix_b, w_ff1, w_ff2, ln_ff_g, ln_ff_b):
    bsz, seq, _ = x.shape
    g_in, b_in = ln_in_g.reshape(1, -1), ln_in_b.reshape(1, -1)
    meta = jnp.zeros((META_ROWS, D_MODEL), F32).at[:N_META].set(meta_tokens.astype(F32))
    hm = _ln_rows(meta, g_in, b_in, META_ROWS).reshape(1, META_ROWS, D_MODEL)
    hx = _ln_rows(x.reshape(bsz * seq, D_MODEL), g_in, b_in, 512).reshape(bsz, seq, D_MODEL)

    rows = lax.broadcasted_iota(jnp.int32, (KV_TILE, KV_TILE), 0)
    cols = lax.broadcasted_iota(jnp.int32, (KV_TILE, KV_TILE), 1)
    ntri = -(rows >= cols).astype(BF16)
    zero_ctail = jnp.zeros((1, CONF_HALO, CONF_WIDTH), F32)
    zero_stail = jnp.zeros((1, SC_HALO, SC_WIDTH), F32)

    for l in range(DEPTH):
        w = _layer_weights(l, w_in, w_conf_dw, b_conf_dw, ln_conf_g, ln_conf_b, w_short_dw, g_mix,
                           w_out, ln_mix_g, ln_mix_b, w_ff1, w_ff2, ln_ff_g, ln_ff_b)
        qm, ktm, vm, ycm, ysm, ctail, stail = _proj(hm, w, zero_ctail, zero_stail,
                                                    tm=META_ROWS, tail_at=N_META)
        ctx = _ctx_operands(ktm, vm)
        om = _attn(qm, ktm, vm, ctx, ntri, n_sub=1, n_ctx=0)
        hm = _post(hm, om, ycm, ysm, w, tm=META_ROWS, n_sub=1)

        q, kt, v, yc, ys, _, _ = _proj(hx, w, ctail, stail, tm=512, tail_at=512)
        o = _attn(q, kt, v, ctx, ntri, n_sub=2, n_ctx=N_META)
        hx = _post(hx, o, yc, ys, w, tm=512, n_sub=2)
    return hx
```

```python
import functools
import math

import jax
import jax.numpy as jnp
from jax import lax
from jax.experimental import pallas as pl
from jax.experimental.pallas import tpu as pltpu

D_MODEL = 1024
DEPTH = 4
N_META = 16
SB_WIDTH = 512
HEAD_DIM = 64
CONF_WIDTH = 256
CONF_KERNEL = 31
SC_WIDTH = 256
SC_KERNEL = 3
D_FF = 4 * D_MODEL
ALPHA = (2.0 * DEPTH) ** 0.25
LN_EPS = 1e-5
RMS_EPS = 1e-6
LOG2E = math.log2(math.e)

LANES = 128
SUBLANES = 8
CONV_CHUNK = 64
SOFTPLUS2_CAP = 64.0
DEAD_LOG2 = -160.0
KV_TILE = 256
CONF_HALO = 32
SC_HALO = 8
META_ROWS = 256
VMEM_LIMIT = 56 * 1024 * 1024

BF16 = jnp.bfloat16
F32 = jnp.float32


def _dot(a, b):
    return jnp.dot(a, b, preferred_element_type=F32)


def _layer_norm(x, g, b):
    mu = jnp.mean(x, axis=-1, keepdims=True)
    xc = x - mu
    var = jnp.mean(xc * xc, axis=-1, keepdims=True)
    return xc * lax.rsqrt(var + LN_EPS) * g + b


def _rms_norm(x, g):
    return x * lax.rsqrt(jnp.mean(x * x, axis=-1, keepdims=True) + RMS_EPS) * g


def _const_spec(shape):
    zeros = (0,) * len(shape)
    return pl.BlockSpec(shape, lambda *_: zeros)


def _ln_kernel(x_ref, g_ref, b_ref, o_ref):
    o_ref[...] = _layer_norm(x_ref[...], g_ref[...], b_ref[...])


def _ln_rows(x2d, g, b, tm):
    rows = x2d.shape[0]
    return pl.pallas_call(
        _ln_kernel,
        grid=(rows // tm,),
        in_specs=[pl.BlockSpec((tm, D_MODEL), lambda i: (i, 0)),
                  _const_spec((1, D_MODEL)), _const_spec((1, D_MODEL))],
        out_specs=pl.BlockSpec((tm, D_MODEL), lambda i: (i, 0)),
        out_shape=jax.ShapeDtypeStruct((rows, D_MODEL), F32),
        compiler_params=pltpu.CompilerParams(dimension_semantics=("arbitrary",),
                                             vmem_limit_bytes=VMEM_LIMIT),
        name="ln_in",
    )(x2d, g, b)


def _proj_kernel(h_ref, wq_ref, wkt_ref, wv_ref, wconf_ref, wsc_ref, wdw_ref, bdw_ref, lng_ref,
                 lnb_ref, wsh_ref, gconf_ref, gsc_ref, ctail_ref, stail_ref,
                 q_ref, kt_ref, v_ref, yc_ref, ys_ref, ctail_out, stail_out,
                 xc_ref, xs_ref, zs_ref, bg_ref, *, tm, tail_at):
    i = pl.program_id(1)

    @pl.when(i == 0)
    def _():
        xc_ref[0:CONF_HALO, :] = ctail_ref[0]
        xs_ref[0:SC_HALO, :] = stail_ref[0]

    hb = h_ref[0].astype(BF16)
    cw, sw = CONF_WIDTH, SC_WIDTH

    xc_ref[CONF_HALO:CONF_HALO + tm, :] = (_dot(hb, wconf_ref[:, 0:cw])
                                           * jax.nn.sigmoid(_dot(hb, wconf_ref[:, cw:2 * cw])))
    ctail_out[0] = xc_ref[tail_at:tail_at + CONF_HALO, :]
    z_rows = tm + CONF_HALO - SUBLANES
    for r in range(SUBLANES):
        zs_ref[r] = xc_ref[SUBLANES - r:SUBLANES - r + z_rows, :]
    bg_ref[...] = _dot(hb, wsc_ref[:, 0:sw])
    xs_ref[SC_HALO:SC_HALO + tm, :] = (_dot(hb, wsc_ref[:, sw:2 * sw])
                                       * _dot(hb, wsc_ref[:, 2 * sw:3 * sw]))
    stail_out[0] = xs_ref[tail_at:tail_at + SC_HALO, :]

    def conf_chunk(t0, rows):
        acc = jnp.broadcast_to(bdw_ref[...], (rows, cw))
        for d in range(CONF_KERNEL):
            a, r = divmod(d, SUBLANES)
            off = t0 + CONF_HALO - SUBLANES - SUBLANES * a
            k = CONF_KERNEL - 1 - d
            acc = acc + wdw_ref[k:k + 1, :] * zs_ref[r, off:off + rows, :]
        y = _layer_norm(acc, lng_ref[...], lnb_ref[...])
        y = y * jax.nn.sigmoid(y)
        yc_ref[0, t0:t0 + rows, :] = _rms_norm(y, gconf_ref[...]).astype(BF16)

    def short_chunk(t0, rows):
        conv = None
        for k in range(SC_KERNEL):
            off = t0 + SC_HALO - (SC_KERNEL - 1) + k
            term = wsh_ref[k:k + 1, :] * xs_ref[off:off + rows, :]
            conv = term if conv is None else conv + term
        ys_ref[0, t0:t0 + rows, :] = _rms_norm(bg_ref[t0:t0 + rows, :] * conv,
                                               gsc_ref[...]).astype(BF16)

    def q_block(c):
        cols = slice(c * KV_TILE, (c + 1) * KV_TILE)
        q_ref[0, :, cols] = _dot(hb, wq_ref[:, cols]).astype(BF16)

    def v_block(c):
        cols = slice(c * KV_TILE, (c + 1) * KV_TILE)
        v_ref[0, :, cols] = _dot(hb, wv_ref[:, cols]).astype(BF16)

    def kt_block(c):
        rows = slice(c * KV_TILE, (c + 1) * KV_TILE)
        kt_ref[0, c] = lax.dot_general(wkt_ref[...], hb[rows], (((1,), (1,)), ((), ())),
                                       preferred_element_type=F32).astype(BF16)

    matmuls = ([functools.partial(q_block, c) for c in range(SB_WIDTH // KV_TILE)]
               + [functools.partial(kt_block, c) for c in range(tm // KV_TILE)]
               + [functools.partial(v_block, c) for c in range(SB_WIDTH // KV_TILE)])
    vpu_work = ([functools.partial(conf_chunk, t0, CONV_CHUNK) for t0 in range(0, tm, CONV_CHUNK)]
                + [functools.partial(short_chunk, t0, 2 * CONV_CHUNK)
                   for t0 in range(0, tm, 2 * CONV_CHUNK)])
    n_m, n_v = len(matmuls), len(vpu_work)
    for k in range(n_m):
        matmuls[k]()
        for job in vpu_work[k * n_v // n_m:(k + 1) * n_v // n_m]:
            job()

    xc_ref[0:CONF_HALO, :] = xc_ref[tm:tm + CONF_HALO, :]
    xs_ref[0:SC_HALO, :] = xs_ref[tm:tm + SC_HALO, :]


def _proj(h, w, ctail, stail, *, tm, tail_at):
    bsz, length, _ = h.shape
    n_kv = length // KV_TILE
    row = lambda width: pl.BlockSpec((1, tm, width), lambda b, i: (b, i, 0))
    in_specs = [
        row(D_MODEL),
        _const_spec((D_MODEL, SB_WIDTH)), _const_spec((SB_WIDTH, D_MODEL)),
        _const_spec((D_MODEL, SB_WIDTH)), _const_spec((D_MODEL, 2 * CONF_WIDTH)),
        _const_spec((D_MODEL, 3 * SC_WIDTH)),
        _const_spec((CONF_KERNEL, CONF_WIDTH)), _const_spec((1, CONF_WIDTH)),
        _const_spec((1, CONF_WIDTH)), _const_spec((1, CONF_WIDTH)),
        _const_spec((SC_KERNEL, SC_WIDTH)), _const_spec((1, CONF_WIDTH)), _const_spec((1, SC_WIDTH)),
        _const_spec((1, CONF_HALO, CONF_WIDTH)), _const_spec((1, SC_HALO, SC_WIDTH)),
    ]
    out_specs = [
        row(SB_WIDTH),
        pl.BlockSpec((1, tm // KV_TILE, SB_WIDTH, KV_TILE), lambda b, i: (b, i, 0, 0)),
        row(SB_WIDTH), row(CONF_WIDTH), row(SC_WIDTH),
        pl.BlockSpec((1, CONF_HALO, CONF_WIDTH), lambda b, i: (b, 0, 0)),
        pl.BlockSpec((1, SC_HALO, SC_WIDTH), lambda b, i: (b, 0, 0)),
    ]
    out_shape = [
        jax.ShapeDtypeStruct((bsz, length, SB_WIDTH), BF16),
        jax.ShapeDtypeStruct((bsz, n_kv, SB_WIDTH, KV_TILE), BF16),
        jax.ShapeDtypeStruct((bsz, length, SB_WIDTH), BF16),
        jax.ShapeDtypeStruct((bsz, length, CONF_WIDTH), BF16),
        jax.ShapeDtypeStruct((bsz, length, SC_WIDTH), BF16),
        jax.ShapeDtypeStruct((bsz, CONF_HALO, CONF_WIDTH), F32),
        jax.ShapeDtypeStruct((bsz, SC_HALO, SC_WIDTH), F32),
    ]
    return pl.pallas_call(
        functools.partial(_proj_kernel, tm=tm, tail_at=tail_at),
        grid=(bsz, length // tm),
        in_specs=in_specs, out_specs=out_specs, out_shape=out_shape,
        scratch_shapes=[pltpu.VMEM((tm + CONF_HALO, CONF_WIDTH), F32),
                        pltpu.VMEM((tm + SC_HALO, SC_WIDTH), F32),
                        pltpu.VMEM((SUBLANES, tm + CONF_HALO - SUBLANES, CONF_WIDTH), F32),
                        pltpu.VMEM((tm, SC_WIDTH), F32)],
        compiler_params=pltpu.CompilerParams(dimension_semantics=("arbitrary", "arbitrary"),
                                             vmem_limit_bytes=VMEM_LIMIT),
        name="proj",
    )(h, w["wq"], w["wkt"], w["wv"], w["wconf"], w["wsc"], w["wdw"], w["bdw"], w["lng"],
      w["lnb"], w["wsh"], w["gconf"], w["gsc"], ctail, stail)


def _softplus2(z):
    return jnp.maximum(z, jnp.log2(1.0 + jnp.exp2(jnp.minimum(z, SOFTPLUS2_CAP))))


def _sb_section(chains, qh_ref, ntri, mask, first, update_r, z_ref, lk_ref, a_ref, acc_ref, r_ref):
    n = len(chains)
    zs, betweens, pvs = {}, {}, {}
    for step in range(n + 3):
        if step < n:
            c, load_kt, _ = chains[step]
            zs[step] = _dot(qh_ref[c], load_kt())
        if 0 <= step - 1 < n:
            c = chains[step - 1][0]
            z = zs.pop(step - 1)
            width = z.shape[1]
            sp = _softplus2(z)
            if mask is not None:
                sp = jnp.where(mask, sp, 0.0)
            z_ref[c, :, 0:width] = z
            lk_ref[c, :, 0:width] = sp.astype(BF16)
            betweens[step - 1] = _dot(lk_ref[c, :, 0:width], ntri)
        if 0 <= step - 2 < n:
            c, _, load_vt = chains[step - 2]
            between = betweens.pop(step - 2)
            log_a = z_ref[c, :, 0:width] + between
            if not first:
                log_a = log_a + jnp.concatenate([r_ref[c]] * (width // LANES), axis=1)
            a = jnp.exp2(log_a)
            if mask is not None:
                a = jnp.where(mask, a, 0.0)
            a_ref[c, :, 0:width] = a.astype(BF16)
            pvs[step - 2] = _dot(a_ref[c, :, 0:width], load_vt())
            if update_r:
                total = jnp.broadcast_to(between[:, 0:1], (between.shape[0], LANES))
                r_ref[c] = total if first else r_ref[c] + total
        if 0 <= step - 3 < n:
            c = chains[step - 3][0]
            pv = pvs.pop(step - 3)
            acc_ref[c] = pv if first else acc_ref[c] + pv


def _attn_kernel(q_ref, kt_ref, v_ref, kctx_ref, vctx_ref, ntri_ctx_ref, ntri_ref, o_ref,
                 qh_ref, z_ref, lk_ref, a_ref, acc_ref, r_ref, *, n_sub, n_ctx):
    i = pl.program_id(1)
    n_pairs = SB_WIDTH // LANES
    n_heads = 2 * n_pairs
    lane = lax.broadcasted_iota(jnp.int32, (KV_TILE, LANES), 1)
    for r in range(n_sub):
        for p in range(n_pairs):
            q2 = q_ref[0, r * KV_TILE:(r + 1) * KV_TILE, p * LANES:(p + 1) * LANES]
            zero = jnp.zeros_like(q2)
            qh_ref[r * n_heads + 2 * p] = jnp.where(lane < HEAD_DIM, q2, zero)
            qh_ref[r * n_heads + 2 * p + 1] = jnp.where(lane >= HEAD_DIM, q2, zero)
    scratch = (z_ref, lk_ref, a_ref, acc_ref, r_ref)

    def chains_for(rows, tile_of_row):
        out = []
        for r in rows:
            j = tile_of_row(r)
            start = pl.multiple_of(j * KV_TILE, KV_TILE)
            for h in range(n_heads):
                lanes = slice((h // 2) * LANES, (h // 2 + 1) * LANES)
                out.append((r * n_heads + h,
                            functools.partial(lambda j, lanes: kt_ref[0, j, lanes, :], j, lanes),
                            functools.partial(lambda start, lanes: v_ref[0, pl.ds(start, KV_TILE), lanes],
                                              start, lanes)))
        return out

    row = lax.broadcasted_iota(jnp.int32, (KV_TILE, KV_TILE), 0)
    col = lax.broadcasted_iota(jnp.int32, (KV_TILE, KV_TILE), 1)
    all_rows = range(n_sub)
    _sb_section(chains_for(all_rows, lambda r: n_sub * i + r), qh_ref, ntri_ref[...], col < row,
                True, True, *scratch)

    for r in all_rows:
        def alive(r=r):
            return jnp.max(r_ref[r * n_heads:(r + 1) * n_heads]) > DEAD_LOG2

        def cond(carry):
            j, live = carry
            return jnp.logical_and(j >= 0, live)

        def body(carry, r=r, alive=alive):
            j, _ = carry
            _sb_section(chains_for([r], lambda _: j), qh_ref, ntri_ref[...], None, False, True,
                        *scratch)
            return j - 1, alive()

        lax.while_loop(cond, body, (n_sub * i + r - 1, alive()))

    for r in all_rows:
        rows_r = slice(r * KV_TILE, (r + 1) * KV_TILE)
        o = jnp.concatenate([jnp.where(lane < HEAD_DIM, acc_ref[r * n_heads + 2 * p],
                                       acc_ref[r * n_heads + 2 * p + 1]) for p in range(n_pairs)], axis=1)
        if n_ctx:
            z = _dot(q_ref[0, rows_r, :], kctx_ref[...])
            sp = _softplus2(z)
            log_a = z + _dot(sp.astype(BF16), ntri_ctx_ref[...])
            for h in range(n_heads):
                log_a = log_a + jnp.where(lane // n_ctx == h, r_ref[r * n_heads + h], 0.0)
            o = o + _dot(jnp.exp2(log_a).astype(BF16), vctx_ref[...])
        o_ref[0, rows_r, :] = o


def _ctx_operands(ktm, vm):
    n_heads = SB_WIDTH // HEAD_DIM
    assert n_heads * N_META == LANES
    kt_ctx = ktm[0, 0, :, :N_META]
    v_ctx = vm[0, :N_META, :]
    head_of_dim = lax.broadcasted_iota(jnp.int32, (SB_WIDTH, LANES), 0) // HEAD_DIM
    head_of_col = lax.broadcasted_iota(jnp.int32, (SB_WIDTH, LANES), 1) // N_META
    same = head_of_dim == head_of_col
    kctx = jnp.where(same, jnp.tile(kt_ctx, (1, n_heads)), 0).astype(BF16)
    vctx = jnp.where(same.T, jnp.tile(v_ctx, (n_heads, 1)), 0).astype(BF16)
    j = lax.broadcasted_iota(jnp.int32, (LANES, LANES), 0)
    s = lax.broadcasted_iota(jnp.int32, (LANES, LANES), 1)
    ntri_ctx = -((j >= s) & (j // N_META == s // N_META)).astype(BF16)
    return kctx, vctx, ntri_ctx


def _attn(q, kt, v, ctx, ntri, *, n_sub, n_ctx):
    bsz, length, _ = q.shape
    tq = n_sub * KV_TILE
    n_kv = length // KV_TILE
    n_chains = n_sub * SB_WIDTH // HEAD_DIM
    return pl.pallas_call(
        functools.partial(_attn_kernel, n_sub=n_sub, n_ctx=n_ctx),
        grid=(bsz, length // tq),
        in_specs=[
            pl.BlockSpec((1, tq, SB_WIDTH), lambda b, i: (b, i, 0)),
            pl.BlockSpec((1, n_kv, SB_WIDTH, KV_TILE), lambda b, i: (b, 0, 0, 0)),
            pl.BlockSpec((1, length, SB_WIDTH), lambda b, i: (b, 0, 0)),
            _const_spec((SB_WIDTH, LANES)), _const_spec((LANES, SB_WIDTH)),
            _const_spec((LANES, LANES)), _const_spec((KV_TILE, KV_TILE)),
        ],
        out_specs=pl.BlockSpec((1, tq, SB_WIDTH), lambda b, i: (b, i, 0)),
        out_shape=jax.ShapeDtypeStruct((bsz, length, SB_WIDTH), F32),
        scratch_shapes=[pltpu.VMEM((n_chains, KV_TILE, LANES), BF16),
                        pltpu.VMEM((n_chains, KV_TILE, KV_TILE), F32),
                        pltpu.VMEM((n_chains, KV_TILE, KV_TILE), BF16),
                        pltpu.VMEM((n_chains, KV_TILE, KV_TILE), BF16),
                        pltpu.VMEM((n_chains, KV_TILE, LANES), F32),
                        pltpu.VMEM((n_chains, KV_TILE, LANES), F32)],
        compiler_params=pltpu.CompilerParams(dimension_semantics=("arbitrary", "arbitrary"),
                                             vmem_limit_bytes=VMEM_LIMIT),
        name="sb_attn",
    )(q, kt, v, *ctx, ntri)


def _post_kernel(h_ref, o_ref, yc_ref, ys_ref, gsb_ref, wout_ref, lmg_ref, lmb_ref,
                 w1_ref, w2_ref, lfg_ref, lfb_ref, out_ref, *, n_sub):
    rows = h_ref.shape[1] // n_sub
    subs = [slice(s * rows, (s + 1) * rows) for s in range(n_sub)]

    def mix(r):
        ysb = _rms_norm(o_ref[0, r, :], gsb_ref[...]).astype(BF16)
        y = jnp.concatenate([ysb, yc_ref[0, r, :], ys_ref[0, r, :]], axis=-1)
        return _layer_norm(ALPHA * h_ref[0, r, :] + _dot(y, wout_ref[...]), lmg_ref[...], lmb_ref[...])

    def ff_in(h1):
        u = jnp.maximum(_dot(h1.astype(BF16), w1_ref[...]), 0.0)
        return (u * u).astype(BF16)

    h1s = [mix(r) for r in subs]
    u2s = [ff_in(h1) for h1 in h1s]
    for r, h1, u2 in zip(subs, h1s, u2s):
        out_ref[0, r, :] = _layer_norm(ALPHA * h1 + _dot(u2, w2_ref[...]), lfg_ref[...], lfb_ref[...])


def _post(h, o, yc, ys, w, *, tm, n_sub):
    bsz, length, _ = h.shape
    row = lambda width: pl.BlockSpec((1, tm, width), lambda b, i: (b, i, 0))
    single = lambda shape: pl.BlockSpec(shape, lambda *_: (0,) * len(shape),
                                        pipeline_mode=pl.Buffered(1))
    return pl.pallas_call(
        functools.partial(_post_kernel, n_sub=n_sub),
        grid=(bsz, length // tm),
        in_specs=[row(D_MODEL), row(SB_WIDTH), row(CONF_WIDTH), row(SC_WIDTH),
                  _const_spec((1, SB_WIDTH)), single((D_MODEL, D_MODEL)),
                  _const_spec((1, D_MODEL)), _const_spec((1, D_MODEL)),
                  single((D_MODEL, D_FF)), single((D_FF, D_MODEL)),
                  _const_spec((1, D_MODEL)), _const_spec((1, D_MODEL))],
        out_specs=row(D_MODEL),
        out_shape=jax.ShapeDtypeStruct((bsz, length, D_MODEL), F32),
        compiler_params=pltpu.CompilerParams(dimension_semantics=("arbitrary", "arbitrary"),
                                             vmem_limit_bytes=VMEM_LIMIT),
        name="post",
    )(h, o, yc, ys, w["gsb"], w["wout"], w["lmg"], w["lmb"], w["w1"], w["w2"], w["lfg"], w["lfb"])


def _layer_weights(l, w_in, w_conf_dw, b_conf_dw, ln_conf_g, ln_conf_b, w_short_dw, g_mix, w_out,
                   ln_mix_g, ln_mix_b, w_ff1, w_ff2, ln_ff_g, ln_ff_b):
    wi = w_in[l]
    q_scale = LOG2E * HEAD_DIM ** -0.5
    r = lambda a: a.reshape(1, -1)
    return dict(
        wq=(wi[:, :SB_WIDTH] * q_scale).astype(BF16),
        wkt=wi[:, SB_WIDTH:2 * SB_WIDTH].T.astype(BF16),
        wv=wi[:, 2 * SB_WIDTH:3 * SB_WIDTH].astype(BF16),
        wconf=wi[:, 3 * SB_WIDTH:3 * SB_WIDTH + 2 * CONF_WIDTH].astype(BF16),
        wsc=wi[:, 3 * SB_WIDTH + 2 * CONF_WIDTH:].astype(BF16),
        wdw=w_conf_dw[l], bdw=r(b_conf_dw[l]), lng=r(ln_conf_g[l]), lnb=r(ln_conf_b[l]),
        wsh=w_short_dw[l],
        gsb=r(g_mix[l, :SB_WIDTH]), gconf=r(g_mix[l, SB_WIDTH:SB_WIDTH + CONF_WIDTH]),
        gsc=r(g_mix[l, SB_WIDTH + CONF_WIDTH:]),
        wout=w_out[l].astype(BF16), lmg=r(ln_mix_g[l]), lmb=r(ln_mix_b[l]),
        w1=w_ff1[l].astype(BF16), w2=w_ff2[l].astype(BF16), lfg=r(ln_ff_g[l]), lfb=r(ln_ff_b[l]),
    )


def kernel(x, meta_tokens, ln_in_g, ln_in_b, w_in, w_conf_dw, b_conf_dw, ln_conf_g, ln_conf_b,
           w_short_dw, g_mix, w_out, ln_mix_g, ln_mix_b, w_ff1, w_ff2, ln_ff_g, ln_ff_b):
    bsz, seq, _ = x.shape
    g_in, b_in = ln_in_g.reshape(1, -1), ln_in_b.reshape(1, -1)
    meta = jnp.zeros((META_ROWS, D_MODEL), F32).at[:N_META].set(meta_tokens.astype(F32))
    hm = _ln_rows(meta, g_in, b_in, META_ROWS).reshape(1, META_ROWS, D_MODEL)
    hx = _ln_rows(x.reshape(bsz * seq, D_MODEL), g_in, b_in, 512).reshape(bsz, seq, D_MODEL)

    rows = lax.broadcasted_iota(jnp.int32, (KV_TILE, KV_TILE), 0)
    cols = lax.broadcasted_iota(jnp.int32, (KV_TILE, KV_TILE), 1)
    ntri = -(rows >= cols).astype(BF16)
    zero_ctail = jnp.zeros((1, CONF_HALO, CONF_WIDTH), F32)
    zero_stail = jnp.zeros((1, SC_HALO, SC_WIDTH), F32)

    for l in range(DEPTH):
        w = _layer_weights(l, w_in, w_conf_dw, b_conf_dw, ln_conf_g, ln_conf_b, w_short_dw, g_mix,
                           w_out, ln_mix_g, ln_mix_b, w_ff1, w_ff2, ln_ff_g, ln_ff_b)
        qm, ktm, vm, ycm, ysm, ctail, stail = _proj(hm, w, zero_ctail, zero_stail,
                                                    tm=META_ROWS, tail_at=N_META)
        ctx = _ctx_operands(ktm, vm)
        om = _attn(qm, ktm, vm, ctx, ntri, n_sub=1, n_ctx=0)
        hm = _post(hm, om, ycm, ysm, w, tm=META_ROWS, n_sub=1)

        q, kt, v, yc, ys, _, _ = _proj(hx, w, ctail, stail, tm=512, tail_at=512)
        o = _attn(q, kt, v, ctx, ntri, n_sub=2, n_ctx=N_META)
        hx = _post(hx, o, yc, ys, w, tm=512, n_sub=2)
    return hx
```

```python
import functools
import math

import jax
import jax.numpy as jnp
from jax import lax
from jax.experimental import pallas as pl
from jax.experimental.pallas import tpu as pltpu

D_MODEL = 1024
DEPTH = 4
N_META = 16
SB_WIDTH = 512
HEAD_DIM = 64
CONF_WIDTH = 256
CONF_KERNEL = 31
SC_WIDTH = 256
SC_KERNEL = 3
D_FF = 4 * D_MODEL
ALPHA = (2.0 * DEPTH) ** 0.25
LN_EPS = 1e-5
RMS_EPS = 1e-6
LOG2E = math.log2(math.e)

LANES = 128
SUBLANES = 8
CONV_CHUNK = 64
SOFTPLUS2_CAP = 64.0
DEAD_LOG2 = -160.0
KV_TILE = 256
CONF_HALO = 32
SC_HALO = 8
META_ROWS = 256
VMEM_LIMIT = 56 * 1024 * 1024

BF16 = jnp.bfloat16
F32 = jnp.float32


def _dot(a, b):
    return jnp.dot(a, b, preferred_element_type=F32)


def _layer_norm(x, g, b):
    mu = jnp.mean(x, axis=-1, keepdims=True)
    xc = x - mu
    var = jnp.mean(xc * xc, axis=-1, keepdims=True)
    return xc * lax.rsqrt(var + LN_EPS) * g + b


def _rms_norm(x, g):
    return x * lax.rsqrt(jnp.mean(x * x, axis=-1, keepdims=True) + RMS_EPS) * g


def _const_spec(shape):
    zeros = (0,) * len(shape)
    return pl.BlockSpec(shape, lambda *_: zeros)


def _proj_kernel(h_ref, gin_ref, bin_ref, wq_ref, wkt_ref, wv_ref, wconf_ref, wsc_ref, wsh_ref,
                 gsc_ref, stail_ref, q_ref, kt_ref, v_ref, hc_ref, ys_ref, stail_out, *rest,
                 tm, tail_at, ln_in):
    i = pl.program_id(1)
    if ln_in:
        h0_ref, xs_ref, bg_ref = rest
    else:
        xs_ref, bg_ref = rest

    @pl.when(i == 0)
    def _():
        xs_ref[0:SC_HALO, :] = stail_ref[0]

    if ln_in:
        h = _layer_norm(h_ref[0], gin_ref[...], bin_ref[...])
        h0_ref[0] = h
    else:
        h = h_ref[0]
    hb = h.astype(BF16)
    cw, sw = CONF_WIDTH, SC_WIDTH

    hc_ref[0] = _dot(hb, wconf_ref[:, 0:cw]) * jax.nn.sigmoid(_dot(hb, wconf_ref[:, cw:2 * cw]))
    bg_ref[...] = _dot(hb, wsc_ref[:, 0:sw])
    xs_ref[SC_HALO:SC_HALO + tm, :] = (_dot(hb, wsc_ref[:, sw:2 * sw])
                                       * _dot(hb, wsc_ref[:, 2 * sw:3 * sw]))
    stail_out[0] = xs_ref[tail_at:tail_at + SC_HALO, :]

    def short_chunk(t0, rows):
        conv = None
        for k in range(SC_KERNEL):
            off = t0 + SC_HALO - (SC_KERNEL - 1) + k
            term = wsh_ref[k:k + 1, :] * xs_ref[off:off + rows, :]
            conv = term if conv is None else conv + term
        ys_ref[0, t0:t0 + rows, :] = _rms_norm(bg_ref[t0:t0 + rows, :] * conv,
                                               gsc_ref[...]).astype(BF16)

    def q_block(c):
        cols = slice(c * KV_TILE, (c + 1) * KV_TILE)
        q_ref[0, :, cols] = _dot(hb, wq_ref[:, cols]).astype(BF16)

    def v_block(c):
        cols = slice(c * KV_TILE, (c + 1) * KV_TILE)
        v_ref[0, :, cols] = _dot(hb, wv_ref[:, cols]).astype(BF16)

    def kt_block(c):
        rows = slice(c * KV_TILE, (c + 1) * KV_TILE)
        kt_ref[0, c] = lax.dot_general(wkt_ref[...], hb[rows], (((1,), (1,)), ((), ())),
                                       preferred_element_type=F32).astype(BF16)

    matmuls = ([functools.partial(q_block, c) for c in range(SB_WIDTH // KV_TILE)]
               + [functools.partial(kt_block, c) for c in range(tm // KV_TILE)]
               + [functools.partial(v_block, c) for c in range(SB_WIDTH // KV_TILE)])
    vpu_work = [functools.partial(short_chunk, t0, 2 * CONV_CHUNK) for t0 in range(0, tm, 2 * CONV_CHUNK)]
    n_m, n_v = len(matmuls), len(vpu_work)
    for k in range(n_m):
        matmuls[k]()
        for job in vpu_work[k * n_v // n_m:(k + 1) * n_v // n_m]:
            job()

    xs_ref[0:SC_HALO, :] = xs_ref[tm:tm + SC_HALO, :]


def _proj(h, ln_in_gb, w, stail, *, tm, tail_at, ln_in):
    bsz, length, _ = h.shape
    n_kv = length // KV_TILE
    row = lambda width: pl.BlockSpec((1, tm, width), lambda b, i: (b, i, 0))
    in_specs = [
        row(D_MODEL), _const_spec((1, D_MODEL)), _const_spec((1, D_MODEL)),
        _const_spec((D_MODEL, SB_WIDTH)), _const_spec((SB_WIDTH, D_MODEL)),
        _const_spec((D_MODEL, SB_WIDTH)), _const_spec((D_MODEL, 2 * CONF_WIDTH)),
        _const_spec((D_MODEL, 3 * SC_WIDTH)),
        _const_spec((SC_KERNEL, SC_WIDTH)), _const_spec((1, SC_WIDTH)),
        _const_spec((1, SC_HALO, SC_WIDTH)),
    ]
    out_specs = [
        row(SB_WIDTH),
        pl.BlockSpec((1, tm // KV_TILE, SB_WIDTH, KV_TILE), lambda b, i: (b, i, 0, 0)),
        row(SB_WIDTH), row(CONF_WIDTH), row(SC_WIDTH),
        pl.BlockSpec((1, SC_HALO, SC_WIDTH), lambda b, i: (b, 0, 0)),
    ]
    out_shape = [
        jax.ShapeDtypeStruct((bsz, length, SB_WIDTH), BF16),
        jax.ShapeDtypeStruct((bsz, n_kv, SB_WIDTH, KV_TILE), BF16),
        jax.ShapeDtypeStruct((bsz, length, SB_WIDTH), BF16),
        jax.ShapeDtypeStruct((bsz, length, CONF_WIDTH), F32),
        jax.ShapeDtypeStruct((bsz, length, SC_WIDTH), BF16),
        jax.ShapeDtypeStruct((bsz, SC_HALO, SC_WIDTH), F32),
    ]
    if ln_in:
        out_specs.append(row(D_MODEL))
        out_shape.append(jax.ShapeDtypeStruct((bsz, length, D_MODEL), F32))
    return pl.pallas_call(
        functools.partial(_proj_kernel, tm=tm, tail_at=tail_at, ln_in=ln_in),
        grid=(bsz, length // tm),
        in_specs=in_specs, out_specs=out_specs, out_shape=out_shape,
        scratch_shapes=[pltpu.VMEM((tm + SC_HALO, SC_WIDTH), F32),
                        pltpu.VMEM((tm, SC_WIDTH), F32)],
        compiler_params=pltpu.CompilerParams(dimension_semantics=("arbitrary", "arbitrary"),
                                             vmem_limit_bytes=VMEM_LIMIT),
        name="proj",
    )(h, *ln_in_gb, w["wq"], w["wkt"], w["wv"], w["wconf"], w["wsc"], w["wsh"], w["gsc"], stail)


def _softplus2(z):
    return jnp.maximum(z, jnp.log2(1.0 + jnp.exp2(jnp.minimum(z, SOFTPLUS2_CAP))))


def _sb_section(chains, qh_ref, ntri, mask, first, update_r, z_ref, lk_ref, a_ref, acc_ref, r_ref):
    n = len(chains)
    zs, betweens, pvs = {}, {}, {}
    for step in range(n + 3):
        if step < n:
            c, load_kt, _ = chains[step]
            zs[step] = _dot(qh_ref[c], load_kt())
        if 0 <= step - 1 < n:
            c = chains[step - 1][0]
            z = zs.pop(step - 1)
            width = z.shape[1]
            sp = _softplus2(z)
            if mask is not None:
                sp = jnp.where(mask, sp, 0.0)
            z_ref[c, :, 0:width] = z
            lk_ref[c, :, 0:width] = sp.astype(BF16)
            betweens[step - 1] = _dot(lk_ref[c, :, 0:width], ntri)
        if 0 <= step - 2 < n:
            c, _, load_vt = chains[step - 2]
            between = betweens.pop(step - 2)
            log_a = z_ref[c, :, 0:width] + between
            if not first:
                log_a = log_a + jnp.concatenate([r_ref[c]] * (width // LANES), axis=1)
            a = jnp.exp2(log_a)
            if mask is not None:
                a = jnp.where(mask, a, 0.0)
            a_ref[c, :, 0:width] = a.astype(BF16)
            pvs[step - 2] = _dot(a_ref[c, :, 0:width], load_vt())
            if update_r:
                total = jnp.broadcast_to(between[:, 0:1], (between.shape[0], LANES))
                r_ref[c] = total if first else r_ref[c] + total
        if 0 <= step - 3 < n:
            c = chains[step - 3][0]
            pv = pvs.pop(step - 3)
            acc_ref[c] = pv if first else acc_ref[c] + pv


def _attn_kernel(q_ref, kt_ref, v_ref, kctx_ref, vctx_ref, ntri_ctx_ref, ntri_ref, o_ref,
                 qh_ref, z_ref, lk_ref, a_ref, acc_ref, r_ref, *, n_sub, n_ctx):
    i = pl.program_id(1)
    n_pairs = SB_WIDTH // LANES
    n_heads = 2 * n_pairs
    lane = lax.broadcasted_iota(jnp.int32, (KV_TILE, LANES), 1)
    for r in range(n_sub):
        for p in range(n_pairs):
            q2 = q_ref[0, r * KV_TILE:(r + 1) * KV_TILE, p * LANES:(p + 1) * LANES]
            zero = jnp.zeros_like(q2)
            qh_ref[r * n_heads + 2 * p] = jnp.where(lane < HEAD_DIM, q2, zero)
            qh_ref[r * n_heads + 2 * p + 1] = jnp.where(lane >= HEAD_DIM, q2, zero)
    scratch = (z_ref, lk_ref, a_ref, acc_ref, r_ref)

    def chains_for(rows, tile_of_row):
        out = []
        for r in rows:
            j = tile_of_row(r)
            start = pl.multiple_of(j * KV_TILE, KV_TILE)
            for h in range(n_heads):
                lanes = slice((h // 2) * LANES, (h // 2 + 1) * LANES)
                out.append((r * n_heads + h,
                            functools.partial(lambda j, lanes: kt_ref[0, j, lanes, :], j, lanes),
                            functools.partial(lambda start, lanes: v_ref[0, pl.ds(start, KV_TILE), lanes],
                                              start, lanes)))
        return out

    row = lax.broadcasted_iota(jnp.int32, (KV_TILE, KV_TILE), 0)
    col = lax.broadcasted_iota(jnp.int32, (KV_TILE, KV_TILE), 1)
    all_rows = range(n_sub)
    _sb_section(chains_for(all_rows, lambda r: n_sub * i + r), qh_ref, ntri_ref[...], col < row,
                True, True, *scratch)

    for r in all_rows:
        def alive(r=r):
            return jnp.max(r_ref[r * n_heads:(r + 1) * n_heads]) > DEAD_LOG2

        def cond(carry):
            j, live = carry
            return jnp.logical_and(j >= 0, live)

        def body(carry, r=r, alive=alive):
            j, _ = carry
            _sb_section(chains_for([r], lambda _: j), qh_ref, ntri_ref[...], None, False, True,
                        *scratch)
            return j - 1, alive()

        lax.while_loop(cond, body, (n_sub * i + r - 1, alive()))

    for r in all_rows:
        rows_r = slice(r * KV_TILE, (r + 1) * KV_TILE)
        o = jnp.concatenate([jnp.where(lane < HEAD_DIM, acc_ref[r * n_heads + 2 * p],
                                       acc_ref[r * n_heads + 2 * p + 1]) for p in range(n_pairs)], axis=1)
        if n_ctx:
            z = _dot(q_ref[0, rows_r, :], kctx_ref[...])
            sp = _softplus2(z)
            log_a = z + _dot(sp.astype(BF16), ntri_ctx_ref[...])
            for h in range(n_heads):
                log_a = log_a + jnp.where(lane // n_ctx == h, r_ref[r * n_heads + h], 0.0)
            o = o + _dot(jnp.exp2(log_a).astype(BF16), vctx_ref[...])
        o_ref[0, rows_r, :] = o


def _ctx_operands(ktm, vm):
    n_heads = SB_WIDTH // HEAD_DIM
    assert n_heads * N_META == LANES
    kt_ctx = ktm[0, 0, :, :N_META]
    v_ctx = vm[0, :N_META, :]
    head_of_dim = lax.broadcasted_iota(jnp.int32, (SB_WIDTH, LANES), 0) // HEAD_DIM
    head_of_col = lax.broadcasted_iota(jnp.int32, (SB_WIDTH, LANES), 1) // N_META
    same = head_of_dim == head_of_col
    kctx = jnp.where(same, jnp.tile(kt_ctx, (1, n_heads)), 0).astype(BF16)
    vctx = jnp.where(same.T, jnp.tile(v_ctx, (n_heads, 1)), 0).astype(BF16)
    j = lax.broadcasted_iota(jnp.int32, (LANES, LANES), 0)
    s = lax.broadcasted_iota(jnp.int32, (LANES, LANES), 1)
    ntri_ctx = -((j >= s) & (j // N_META == s // N_META)).astype(BF16)
    return kctx, vctx, ntri_ctx


def _attn(q, kt, v, ctx, ntri, *, n_sub, n_ctx):
    bsz, length, _ = q.shape
    tq = n_sub * KV_TILE
    n_kv = length // KV_TILE
    n_chains = n_sub * SB_WIDTH // HEAD_DIM
    return pl.pallas_call(
        functools.partial(_attn_kernel, n_sub=n_sub, n_ctx=n_ctx),
        grid=(bsz, length // tq),
        in_specs=[
            pl.BlockSpec((1, tq, SB_WIDTH), lambda b, i: (b, i, 0)),
            pl.BlockSpec((1, n_kv, SB_WIDTH, KV_TILE), lambda b, i: (b, 0, 0, 0)),
            pl.BlockSpec((1, length, SB_WIDTH), lambda b, i: (b, 0, 0)),
            _const_spec((SB_WIDTH, LANES)), _const_spec((LANES, SB_WIDTH)),
            _const_spec((LANES, LANES)), _const_spec((KV_TILE, KV_TILE)),
        ],
        out_specs=pl.BlockSpec((1, tq, SB_WIDTH), lambda b, i: (b, i, 0)),
        out_shape=jax.ShapeDtypeStruct((bsz, length, SB_WIDTH), F32),
        scratch_shapes=[pltpu.VMEM((n_chains, KV_TILE, LANES), BF16),
                        pltpu.VMEM((n_chains, KV_TILE, KV_TILE), F32),
                        pltpu.VMEM((n_chains, KV_TILE, KV_TILE), BF16),
                        pltpu.VMEM((n_chains, KV_TILE, KV_TILE), BF16),
                        pltpu.VMEM((n_chains, KV_TILE, LANES), F32),
                        pltpu.VMEM((n_chains, KV_TILE, LANES), F32)],
        compiler_params=pltpu.CompilerParams(dimension_semantics=("arbitrary", "arbitrary"),
                                             vmem_limit_bytes=VMEM_LIMIT),
        name="sb_attn",
    )(q, kt, v, *ctx, ntri)


def _post_kernel(h_ref, o_ref, hc_ref, ys_ref, ctail_ref, gsb_ref, wdw_ref, bdw_ref, lng_ref, lnb_ref,
                 gconf_ref, wout_ref, lmg_ref, lmb_ref, w1_ref, w2_ref, lfg_ref, lfb_ref,
                 out_ref, ctail_out, xc_ref, zs_ref, *, n_sub, tail_at):
    i = pl.program_id(1)
    tm = h_ref.shape[1]
    rows = tm // n_sub
    subs = [slice(s * rows, (s + 1) * rows) for s in range(n_sub)]
    sb, cw = SB_WIDTH, CONF_WIDTH

    @pl.when(i == 0)
    def _():
        xc_ref[0:CONF_HALO, :] = ctail_ref[0]

    xc_ref[CONF_HALO:CONF_HALO + tm, :] = hc_ref[0]
    ctail_out[0] = xc_ref[tail_at:tail_at + CONF_HALO, :]
    z_rows = tm + CONF_HALO - SUBLANES
    for r in range(SUBLANES):
        zs_ref[r] = xc_ref[SUBLANES - r:SUBLANES - r + z_rows, :]

    def conf_chunk(t0, n):
        acc = jnp.broadcast_to(bdw_ref[...], (n, cw))
        for d in range(CONF_KERNEL):
            a, r = divmod(d, SUBLANES)
            off = t0 + CONF_HALO - SUBLANES - SUBLANES * a
            k = CONF_KERNEL - 1 - d
            acc = acc + wdw_ref[k:k + 1, :] * zs_ref[r, off:off + n, :]
        y = _layer_norm(acc, lng_ref[...], lnb_ref[...])
        y = y * jax.nn.sigmoid(y)
        return _rms_norm(y, gconf_ref[...]).astype(BF16)

    def mix(r):
        ysb = _rms_norm(o_ref[0, r, :], gsb_ref[...]).astype(BF16)
        part = _dot(ysb, wout_ref[0:sb, :]) + _dot(ys_ref[0, r, :], wout_ref[sb + cw:, :])
        yc = jnp.concatenate([conf_chunk(t0, CONV_CHUNK)
                              for t0 in range(r.start, r.stop, CONV_CHUNK)], axis=0)
        mixed = part + _dot(yc, wout_ref[sb:sb + cw, :])
        return _layer_norm(ALPHA * h_ref[0, r, :] + mixed, lmg_ref[...], lmb_ref[...])

    def ff_in(h1):
        u = jnp.maximum(_dot(h1.astype(BF16), w1_ref[...]), 0.0)
        return (u * u).astype(BF16)

    h1s = [mix(r) for r in subs]
    u2s = [ff_in(h1) for h1 in h1s]
    for r, h1, u2 in zip(subs, h1s, u2s):
        out_ref[0, r, :] = _layer_norm(ALPHA * h1 + _dot(u2, w2_ref[...]), lfg_ref[...], lfb_ref[...])

    xc_ref[0:CONF_HALO, :] = xc_ref[tm:tm + CONF_HALO, :]


def _post(h, o, hc, ys, ctail, w, *, tm, n_sub, tail_at):
    bsz, length, _ = h.shape
    row = lambda width: pl.BlockSpec((1, tm, width), lambda b, i: (b, i, 0))
    single = lambda shape: pl.BlockSpec(shape, lambda *_: (0,) * len(shape),
                                        pipeline_mode=pl.Buffered(1))
    tail_spec = lambda index_map: pl.BlockSpec((1, CONF_HALO, CONF_WIDTH), index_map)
    return pl.pallas_call(
        functools.partial(_post_kernel, n_sub=n_sub, tail_at=tail_at),
        grid=(bsz, length // tm),
        in_specs=[row(D_MODEL), row(SB_WIDTH), row(CONF_WIDTH), row(SC_WIDTH),
                  tail_spec(lambda b, i: (0, 0, 0)), _const_spec((1, SB_WIDTH)),
                  _const_spec((CONF_KERNEL, CONF_WIDTH)), _const_spec((1, CONF_WIDTH)),
                  _const_spec((1, CONF_WIDTH)), _const_spec((1, CONF_WIDTH)),
                  _const_spec((1, CONF_WIDTH)), single((D_MODEL, D_MODEL)),
                  _const_spec((1, D_MODEL)), _const_spec((1, D_MODEL)),
                  single((D_MODEL, D_FF)), single((D_FF, D_MODEL)),
                  _const_spec((1, D_MODEL)), _const_spec((1, D_MODEL))],
        out_specs=[row(D_MODEL), tail_spec(lambda b, i: (b, 0, 0))],
        out_shape=[jax.ShapeDtypeStruct((bsz, length, D_MODEL), F32),
                   jax.ShapeDtypeStruct((bsz, CONF_HALO, CONF_WIDTH), F32)],
        scratch_shapes=[pltpu.VMEM((tm + CONF_HALO, CONF_WIDTH), F32),
                        pltpu.VMEM((SUBLANES, tm + CONF_HALO - SUBLANES, CONF_WIDTH), F32)],
        compiler_params=pltpu.CompilerParams(dimension_semantics=("arbitrary", "arbitrary"),
                                             vmem_limit_bytes=VMEM_LIMIT),
        name="post",
    )(h, o, hc, ys, ctail, w["gsb"], w["wdw"], w["bdw"], w["lng"], w["lnb"], w["gconf"],
      w["wout"], w["lmg"], w["lmb"], w["w1"], w["w2"], w["lfg"], w["lfb"])


def _layer_weights(l, w_in, w_conf_dw, b_conf_dw, ln_conf_g, ln_conf_b, w_short_dw, g_mix, w_out,
                   ln_mix_g, ln_mix_b, w_ff1, w_ff2, ln_ff_g, ln_ff_b):
    wi = w_in[l]
    q_scale = LOG2E * HEAD_DIM ** -0.5
    r = lambda a: a.reshape(1, -1)
    return dict(
        wq=(wi[:, :SB_WIDTH] * q_scale).astype(BF16),
        wkt=wi[:, SB_WIDTH:2 * SB_WIDTH].T.astype(BF16),
        wv=wi[:, 2 * SB_WIDTH:3 * SB_WIDTH].astype(BF16),
        wconf=wi[:, 3 * SB_WIDTH:3 * SB_WIDTH + 2 * CONF_WIDTH].astype(BF16),
        wsc=wi[:, 3 * SB_WIDTH + 2 * CONF_WIDTH:].astype(BF16),
        wdw=w_conf_dw[l], bdw=r(b_conf_dw[l]), lng=r(ln_conf_g[l]), lnb=r(ln_conf_b[l]),
        wsh=w_short_dw[l],
        gsb=r(g_mix[l, :SB_WIDTH]), gconf=r(g_mix[l, SB_WIDTH:SB_WIDTH + CONF_WIDTH]),
        gsc=r(g_mix[l, SB_WIDTH + CONF_WIDTH:]),
        wout=w_out[l].astype(BF16), lmg=r(ln_mix_g[l]), lmb=r(ln_mix_b[l]),
        w1=w_ff1[l].astype(BF16), w2=w_ff2[l].astype(BF16), lfg=r(ln_ff_g[l]), lfb=r(ln_ff_b[l]),
    )


def kernel(x, meta_tokens, ln_in_g, ln_in_b, w_in, w_conf_dw, b_conf_dw, ln_conf_g, ln_conf_b,
           w_short_dw, g_mix, w_out, ln_mix_g, ln_mix_b, w_ff1, w_ff2, ln_ff_g, ln_ff_b):
    bsz, seq, _ = x.shape
    ln_in_gb = (ln_in_g.reshape(1, -1), ln_in_b.reshape(1, -1))
    hm = jnp.zeros((1, META_ROWS, D_MODEL), F32).at[0, :N_META].set(meta_tokens.astype(F32))
    hx = x

    rows = lax.broadcasted_iota(jnp.int32, (KV_TILE, KV_TILE), 0)
    cols = lax.broadcasted_iota(jnp.int32, (KV_TILE, KV_TILE), 1)
    ntri = -(rows >= cols).astype(BF16)
    zero_ctail = jnp.zeros((1, CONF_HALO, CONF_WIDTH), F32)
    zero_stail = jnp.zeros((1, SC_HALO, SC_WIDTH), F32)

    for l in range(DEPTH):
        w = _layer_weights(l, w_in, w_conf_dw, b_conf_dw, ln_conf_g, ln_conf_b, w_short_dw, g_mix,
                           w_out, ln_mix_g, ln_mix_b, w_ff1, w_ff2, ln_ff_g, ln_ff_b)
        first = l == 0
        qm, ktm, vm, hcm, ysm, stail, *hm0 = _proj(hm, ln_in_gb, w, zero_stail, tm=META_ROWS,
                                                   tail_at=N_META, ln_in=first)
        q, kt, v, hc, ys, _, *hx0 = _proj(hx, ln_in_gb, w, stail, tm=512, tail_at=512, ln_in=first)
        if first:
            (hm,), (hx,) = hm0, hx0
        ctx = _ctx_operands(ktm, vm)
        om = _attn(qm, ktm, vm, ctx, ntri, n_sub=1, n_ctx=0)
        hm, ctail = _post(hm, om, hcm, ysm, zero_ctail, w, tm=META_ROWS, n_sub=1, tail_at=N_META)
        o = _attn(q, kt, v, ctx, ntri, n_sub=2, n_ctx=N_META)
        hx, _ = _post(hx, o, hc, ys, ctail, w, tm=512, n_sub=2, tail_at=512)
    return hx
```

```python
import functools
import math

import jax
import jax.numpy as jnp
from jax import lax
from jax.experimental import pallas as pl
from jax.experimental.pallas import tpu as pltpu

D_MODEL = 1024
DEPTH = 4
N_META = 16
SB_WIDTH = 512
HEAD_DIM = 64
CONF_WIDTH = 256
CONF_KERNEL = 31
SC_WIDTH = 256
SC_KERNEL = 3
D_FF = 4 * D_MODEL
ALPHA = (2.0 * DEPTH) ** 0.25
LN_EPS = 1e-5
RMS_EPS = 1e-6
LOG2E = math.log2(math.e)

LANES = 128
SUBLANES = 8
CONV_CHUNK = 64
SOFTPLUS2_CAP = 64.0
DEAD_LOG2 = -160.0
KV_TILE = 256
CONF_HALO = 32
SC_HALO = 8
META_ROWS = 256
VMEM_LIMIT = 56 * 1024 * 1024

BF16 = jnp.bfloat16
F32 = jnp.float32


def _dot(a, b):
    return jnp.dot(a, b, preferred_element_type=F32)


def _layer_norm(x, g, b):
    mu = jnp.mean(x, axis=-1, keepdims=True)
    xc = x - mu
    var = jnp.mean(xc * xc, axis=-1, keepdims=True)
    return xc * lax.rsqrt(var + LN_EPS) * g + b


def _rms_norm(x, g):
    return x * lax.rsqrt(jnp.mean(x * x, axis=-1, keepdims=True) + RMS_EPS) * g


def _const_spec(shape):
    zeros = (0,) * len(shape)
    return pl.BlockSpec(shape, lambda *_: zeros)


def _proj_kernel(h_ref, gin_ref, bin_ref, wq_ref, wkt_ref, wv_ref, wconf_ref, wsc_ref, wsh_ref,
                 gsc_ref, stail_ref, q_ref, kt_ref, v_ref, hc_ref, ys_ref, stail_out, *rest,
                 tm, tail_at, ln_in):
    i = pl.program_id(1)
    if ln_in:
        h0_ref, xs_ref, bg_ref = rest
    else:
        xs_ref, bg_ref = rest

    @pl.when(i == 0)
    def _():
        xs_ref[0:SC_HALO, :] = stail_ref[0]

    if ln_in:
        h = _layer_norm(h_ref[0], gin_ref[...], bin_ref[...])
        h0_ref[0] = h
    else:
        h = h_ref[0]
    hb = h.astype(BF16)
    cw, sw = CONF_WIDTH, SC_WIDTH

    hc_ref[0] = _dot(hb, wconf_ref[:, 0:cw]) * jax.nn.sigmoid(_dot(hb, wconf_ref[:, cw:2 * cw]))
    bg_ref[...] = _dot(hb, wsc_ref[:, 0:sw])
    xs_ref[SC_HALO:SC_HALO + tm, :] = (_dot(hb, wsc_ref[:, sw:2 * sw])
                                       * _dot(hb, wsc_ref[:, 2 * sw:3 * sw]))
    stail_out[0] = xs_ref[tail_at:tail_at + SC_HALO, :]

    def short_chunk(t0, rows):
        conv = None
        for k in range(SC_KERNEL):
            off = t0 + SC_HALO - (SC_KERNEL - 1) + k
            term = wsh_ref[k:k + 1, :] * xs_ref[off:off + rows, :]
            conv = term if conv is None else conv + term
        ys_ref[0, t0:t0 + rows, :] = _rms_norm(bg_ref[t0:t0 + rows, :] * conv,
                                               gsc_ref[...]).astype(BF16)

    def q_block(c):
        cols = slice(c * KV_TILE, (c + 1) * KV_TILE)
        q_ref[0, :, cols] = _dot(hb, wq_ref[:, cols]).astype(BF16)

    def v_block(c):
        cols = slice(c * KV_TILE, (c + 1) * KV_TILE)
        v_ref[0, :, cols] = _dot(hb, wv_ref[:, cols]).astype(BF16)

    def kt_block(c):
        rows = slice(c * KV_TILE, (c + 1) * KV_TILE)
        kt_ref[0, c] = lax.dot_general(wkt_ref[...], hb[rows], (((1,), (1,)), ((), ())),
                                       preferred_element_type=F32).astype(BF16)

    matmuls = ([functools.partial(q_block, c) for c in range(SB_WIDTH // KV_TILE)]
               + [functools.partial(kt_block, c) for c in range(tm // KV_TILE)]
               + [functools.partial(v_block, c) for c in range(SB_WIDTH // KV_TILE)])
    vpu_work = [functools.partial(short_chunk, t0, 2 * CONV_CHUNK) for t0 in range(0, tm, 2 * CONV_CHUNK)]
    n_m, n_v = len(matmuls), len(vpu_work)
    for k in range(n_m):
        matmuls[k]()
        for job in vpu_work[k * n_v // n_m:(k + 1) * n_v // n_m]:
            job()

    xs_ref[0:SC_HALO, :] = xs_ref[tm:tm + SC_HALO, :]


def _proj(h, ln_in_gb, w, stail, *, tm, tail_at, ln_in):
    bsz, length, _ = h.shape
    n_kv = length // KV_TILE
    row = lambda width: pl.BlockSpec((1, tm, width), lambda b, i: (b, i, 0))
    in_specs = [
        row(D_MODEL), _const_spec((1, D_MODEL)), _const_spec((1, D_MODEL)),
        _const_spec((D_MODEL, SB_WIDTH)), _const_spec((SB_WIDTH, D_MODEL)),
        _const_spec((D_MODEL, SB_WIDTH)), _const_spec((D_MODEL, 2 * CONF_WIDTH)),
        _const_spec((D_MODEL, 3 * SC_WIDTH)),
        _const_spec((SC_KERNEL, SC_WIDTH)), _const_spec((1, SC_WIDTH)),
        _const_spec((1, SC_HALO, SC_WIDTH)),
    ]
    out_specs = [
        row(SB_WIDTH),
        pl.BlockSpec((1, tm // KV_TILE, SB_WIDTH, KV_TILE), lambda b, i: (b, i, 0, 0)),
        row(SB_WIDTH), row(CONF_WIDTH), row(SC_WIDTH),
        pl.BlockSpec((1, SC_HALO, SC_WIDTH), lambda b, i: (b, 0, 0)),
    ]
    out_shape = [
        jax.ShapeDtypeStruct((bsz, length, SB_WIDTH), BF16),
        jax.ShapeDtypeStruct((bsz, n_kv, SB_WIDTH, KV_TILE), BF16),
        jax.ShapeDtypeStruct((bsz, length, SB_WIDTH), BF16),
        jax.ShapeDtypeStruct((bsz, length, CONF_WIDTH), F32),
        jax.ShapeDtypeStruct((bsz, length, SC_WIDTH), BF16),
        jax.ShapeDtypeStruct((bsz, SC_HALO, SC_WIDTH), F32),
    ]
    if ln_in:
        out_specs.append(row(D_MODEL))
        out_shape.append(jax.ShapeDtypeStruct((bsz, length, D_MODEL), F32))
    return pl.pallas_call(
        functools.partial(_proj_kernel, tm=tm, tail_at=tail_at, ln_in=ln_in),
        grid=(bsz, length // tm),
        in_specs=in_specs, out_specs=out_specs, out_shape=out_shape,
        scratch_shapes=[pltpu.VMEM((tm + SC_HALO, SC_WIDTH), F32),
                        pltpu.VMEM((tm, SC_WIDTH), F32)],
        compiler_params=pltpu.CompilerParams(dimension_semantics=("arbitrary", "arbitrary"),
                                             vmem_limit_bytes=VMEM_LIMIT),
        name="proj",
    )(h, *ln_in_gb, w["wq"], w["wkt"], w["wv"], w["wconf"], w["wsc"], w["wsh"], w["gsc"], stail)


def _softplus2(z):
    return jnp.maximum(z, jnp.log2(1.0 + jnp.exp2(jnp.minimum(z, SOFTPLUS2_CAP))))


def _sb_section(chains, qh_ref, ntri, update_r, z_ref, lk_ref, a_ref, acc_ref, r_ref):
    n = len(chains)
    zs, betweens, pvs = {}, {}, {}
    for step in range(n + 3):
        if step < n:
            c, load_kt = chains[step][:2]
            zs[step] = _dot(qh_ref[c], load_kt())
        if 0 <= step - 1 < n:
            c, _, _, mask, _ = chains[step - 1]
            z = zs.pop(step - 1)
            width = z.shape[1]
            sp = _softplus2(z)
            if mask is not None:
                sp = jnp.where(mask, sp, 0.0)
            z_ref[c, :, 0:width] = z
            lk_ref[c, :, 0:width] = sp.astype(BF16)
            betweens[step - 1] = _dot(lk_ref[c, :, 0:width], ntri)
        if 0 <= step - 2 < n:
            c, _, load_vt, mask, first = chains[step - 2]
            between = betweens.pop(step - 2)
            log_a = z_ref[c, :, 0:width] + between
            if not first:
                log_a = log_a + jnp.concatenate([r_ref[c]] * (width // LANES), axis=1)
            a = jnp.exp2(log_a)
            if mask is not None:
                a = jnp.where(mask, a, 0.0)
            a_ref[c, :, 0:width] = a.astype(BF16)
            pvs[step - 2] = _dot(a_ref[c, :, 0:width], load_vt())
            if update_r:
                total = jnp.broadcast_to(between[:, 0:1], (between.shape[0], LANES))
                r_ref[c] = total if first else r_ref[c] + total
        if 0 <= step - 3 < n:
            c, first = chains[step - 3][0], chains[step - 3][4]
            pv = pvs.pop(step - 3)
            acc_ref[c] = pv if first else acc_ref[c] + pv


def _attn_kernel(q_ref, kt_ref, v_ref, kctx_ref, vctx_ref, ntri_ctx_ref, ntri_ref, o_ref,
                 qh_ref, z_ref, lk_ref, a_ref, acc_ref, r_ref, *, n_sub, n_ctx, n_tiles):
    i = pl.program_id(1)
    n_pairs = SB_WIDTH // LANES
    n_heads = 2 * n_pairs
    lane = lax.broadcasted_iota(jnp.int32, (KV_TILE, LANES), 1)
    for r in range(n_sub):
        for p in range(n_pairs):
            q2 = q_ref[0, r * KV_TILE:(r + 1) * KV_TILE, p * LANES:(p + 1) * LANES]
            zero = jnp.zeros_like(q2)
            qh_ref[r * n_heads + 2 * p] = jnp.where(lane < HEAD_DIM, q2, zero)
            qh_ref[r * n_heads + 2 * p + 1] = jnp.where(lane >= HEAD_DIM, q2, zero)
    scratch = (z_ref, lk_ref, a_ref, acc_ref, r_ref)

    def chains_for(rows, tile_of_row, mask_of_row, first):
        out = []
        for r in rows:
            j = tile_of_row(r)
            start = pl.multiple_of(j * KV_TILE, KV_TILE)
            for h in range(n_heads):
                lanes = slice((h // 2) * LANES, (h // 2 + 1) * LANES)
                out.append((r * n_heads + h,
                            functools.partial(lambda j, lanes: kt_ref[0, j, lanes, :], j, lanes),
                            functools.partial(lambda start, lanes: v_ref[0, pl.ds(start, KV_TILE), lanes],
                                              start, lanes),
                            mask_of_row(r), first))
        return out

    row = lax.broadcasted_iota(jnp.int32, (KV_TILE, KV_TILE), 0)
    col = lax.broadcasted_iota(jnp.int32, (KV_TILE, KV_TILE), 1)
    causal = col < row
    all_rows = range(n_sub)
    has_prev = [r for r in all_rows if n_tiles > 1 or r > 0]
    _sb_section(chains_for(all_rows, lambda r: n_sub * i + r, lambda r: causal, True)
                + chains_for(has_prev, lambda r: jnp.maximum(n_sub * i + r - 1, 0),
                             lambda r: None if r > 0 else jnp.broadcast_to(i > 0, causal.shape), False),
                qh_ref, ntri_ref[...], True, *scratch)

    for r in all_rows:
        def alive(r=r):
            return jnp.max(r_ref[r * n_heads:(r + 1) * n_heads]) > DEAD_LOG2

        def cond(carry):
            j, live = carry
            return jnp.logical_and(j >= 0, live)

        def body(carry, r=r, alive=alive):
            j, _ = carry
            _sb_section(chains_for([r], lambda _: j, lambda _: None, False), qh_ref, ntri_ref[...],
                        True, *scratch)
            return j - 1, alive()

        lax.while_loop(cond, body, (n_sub * i + r - 2, alive()))

    for r in all_rows:
        rows_r = slice(r * KV_TILE, (r + 1) * KV_TILE)
        o = jnp.concatenate([jnp.where(lane < HEAD_DIM, acc_ref[r * n_heads + 2 * p],
                                       acc_ref[r * n_heads + 2 * p + 1]) for p in range(n_pairs)], axis=1)
        if n_ctx:
            z = _dot(q_ref[0, rows_r, :], kctx_ref[...])
            sp = _softplus2(z)
            log_a = z + _dot(sp.astype(BF16), ntri_ctx_ref[...])
            for h in range(n_heads):
                log_a = log_a + jnp.where(lane // n_ctx == h, r_ref[r * n_heads + h], 0.0)
            o = o + _dot(jnp.exp2(log_a).astype(BF16), vctx_ref[...])
        o_ref[0, rows_r, :] = o


def _ctx_operands(ktm, vm):
    n_heads = SB_WIDTH // HEAD_DIM
    assert n_heads * N_META == LANES
    kt_ctx = ktm[0, 0, :, :N_META]
    v_ctx = vm[0, :N_META, :]
    head_of_dim = lax.broadcasted_iota(jnp.int32, (SB_WIDTH, LANES), 0) // HEAD_DIM
    head_of_col = lax.broadcasted_iota(jnp.int32, (SB_WIDTH, LANES), 1) // N_META
    same = head_of_dim == head_of_col
    kctx = jnp.where(same, jnp.tile(kt_ctx, (1, n_heads)), 0).astype(BF16)
    vctx = jnp.where(same.T, jnp.tile(v_ctx, (n_heads, 1)), 0).astype(BF16)
    j = lax.broadcasted_iota(jnp.int32, (LANES, LANES), 0)
    s = lax.broadcasted_iota(jnp.int32, (LANES, LANES), 1)
    ntri_ctx = -((j >= s) & (j // N_META == s // N_META)).astype(BF16)
    return kctx, vctx, ntri_ctx


def _attn(q, kt, v, ctx, ntri, *, n_sub, n_ctx):
    bsz, length, _ = q.shape
    tq = n_sub * KV_TILE
    n_kv = length // KV_TILE
    n_chains = n_sub * SB_WIDTH // HEAD_DIM
    return pl.pallas_call(
        functools.partial(_attn_kernel, n_sub=n_sub, n_ctx=n_ctx, n_tiles=length // tq),
        grid=(bsz, length // tq),
        in_specs=[
            pl.BlockSpec((1, tq, SB_WIDTH), lambda b, i: (b, i, 0)),
            pl.BlockSpec((1, n_kv, SB_WIDTH, KV_TILE), lambda b, i: (b, 0, 0, 0)),
            pl.BlockSpec((1, length, SB_WIDTH), lambda b, i: (b, 0, 0)),
            _const_spec((SB_WIDTH, LANES)), _const_spec((LANES, SB_WIDTH)),
            _const_spec((LANES, LANES)), _const_spec((KV_TILE, KV_TILE)),
        ],
        out_specs=pl.BlockSpec((1, tq, SB_WIDTH), lambda b, i: (b, i, 0)),
        out_shape=jax.ShapeDtypeStruct((bsz, length, SB_WIDTH), F32),
        scratch_shapes=[pltpu.VMEM((n_chains, KV_TILE, LANES), BF16),
                        pltpu.VMEM((n_chains, KV_TILE, KV_TILE), F32),
                        pltpu.VMEM((n_chains, KV_TILE, KV_TILE), BF16),
                        pltpu.VMEM((n_chains, KV_TILE, KV_TILE), BF16),
                        pltpu.VMEM((n_chains, KV_TILE, LANES), F32),
                        pltpu.VMEM((n_chains, KV_TILE, LANES), F32)],
        compiler_params=pltpu.CompilerParams(dimension_semantics=("arbitrary", "arbitrary"),
                                             vmem_limit_bytes=VMEM_LIMIT),
        name="sb_attn",
    )(q, kt, v, *ctx, ntri)


def _post_kernel(h_ref, o_ref, hc_ref, ys_ref, ctail_ref, gsb_ref, wdw_ref, bdw_ref, lng_ref, lnb_ref,
                 gconf_ref, wout_ref, lmg_ref, lmb_ref, w1_ref, w2_ref, lfg_ref, lfb_ref,
                 out_ref, ctail_out, xc_ref, zs_ref, *, n_sub, tail_at):
    i = pl.program_id(1)
    tm = h_ref.shape[1]
    rows = tm // n_sub
    subs = [slice(s * rows, (s + 1) * rows) for s in range(n_sub)]
    sb, cw = SB_WIDTH, CONF_WIDTH

    @pl.when(i == 0)
    def _():
        xc_ref[0:CONF_HALO, :] = ctail_ref[0]

    xc_ref[CONF_HALO:CONF_HALO + tm, :] = hc_ref[0]
    ctail_out[0] = xc_ref[tail_at:tail_at + CONF_HALO, :]
    z_rows = tm + CONF_HALO - SUBLANES
    for r in range(SUBLANES):
        zs_ref[r] = xc_ref[SUBLANES - r:SUBLANES - r + z_rows, :]

    def conf_chunk(t0, n):
        acc = jnp.broadcast_to(bdw_ref[...], (n, cw))
        for d in range(CONF_KERNEL):
            a, r = divmod(d, SUBLANES)
            off = t0 + CONF_HALO - SUBLANES - SUBLANES * a
            k = CONF_KERNEL - 1 - d
            acc = acc + wdw_ref[k:k + 1, :] * zs_ref[r, off:off + n, :]
        y = _layer_norm(acc, lng_ref[...], lnb_ref[...])
        y = y * jax.nn.sigmoid(y)
        return _rms_norm(y, gconf_ref[...]).astype(BF16)

    def mix(r):
        ysb = _rms_norm(o_ref[0, r, :], gsb_ref[...]).astype(BF16)
        part = _dot(ysb, wout_ref[0:sb, :]) + _dot(ys_ref[0, r, :], wout_ref[sb + cw:, :])
        yc = jnp.concatenate([conf_chunk(t0, CONV_CHUNK)
                              for t0 in range(r.start, r.stop, CONV_CHUNK)], axis=0)
        mixed = part + _dot(yc, wout_ref[sb:sb + cw, :])
        return _layer_norm(ALPHA * h_ref[0, r, :] + mixed, lmg_ref[...], lmb_ref[...])

    def ff_in(h1):
        u = jnp.maximum(_dot(h1.astype(BF16), w1_ref[...]), 0.0)
        return (u * u).astype(BF16)

    h1s = [mix(r) for r in subs]
    u2s = [ff_in(h1) for h1 in h1s]
    for r, h1, u2 in zip(subs, h1s, u2s):
        out_ref[0, r, :] = _layer_norm(ALPHA * h1 + _dot(u2, w2_ref[...]), lfg_ref[...], lfb_ref[...])

    xc_ref[0:CONF_HALO, :] = xc_ref[tm:tm + CONF_HALO, :]


def _post(h, o, hc, ys, ctail, w, *, tm, n_sub, tail_at):
    bsz, length, _ = h.shape
    row = lambda width: pl.BlockSpec((1, tm, width), lambda b, i: (b, i, 0))
    single = lambda shape: pl.BlockSpec(shape, lambda *_: (0,) * len(shape),
                                        pipeline_mode=pl.Buffered(1))
    tail_spec = lambda index_map: pl.BlockSpec((1, CONF_HALO, CONF_WIDTH), index_map)
    return pl.pallas_call(
        functools.partial(_post_kernel, n_sub=n_sub, tail_at=tail_at),
        grid=(bsz, length // tm),
        in_specs=[row(D_MODEL), row(SB_WIDTH), row(CONF_WIDTH), row(SC_WIDTH),
                  tail_spec(lambda b, i: (0, 0, 0)), _const_spec((1, SB_WIDTH)),
                  _const_spec((CONF_KERNEL, CONF_WIDTH)), _const_spec((1, CONF_WIDTH)),
                  _const_spec((1, CONF_WIDTH)), _const_spec((1, CONF_WIDTH)),
                  _const_spec((1, CONF_WIDTH)), single((D_MODEL, D_MODEL)),
                  _const_spec((1, D_MODEL)), _const_spec((1, D_MODEL)),
                  single((D_MODEL, D_FF)), single((D_FF, D_MODEL)),
                  _const_spec((1, D_MODEL)), _const_spec((1, D_MODEL))],
        out_specs=[row(D_MODEL), tail_spec(lambda b, i: (b, 0, 0))],
        out_shape=[jax.ShapeDtypeStruct((bsz, length, D_MODEL), F32),
                   jax.ShapeDtypeStruct((bsz, CONF_HALO, CONF_WIDTH), F32)],
        scratch_shapes=[pltpu.VMEM((tm + CONF_HALO, CONF_WIDTH), F32),
                        pltpu.VMEM((SUBLANES, tm + CONF_HALO - SUBLANES, CONF_WIDTH), F32)],
        compiler_params=pltpu.CompilerParams(dimension_semantics=("arbitrary", "arbitrary"),
                                             vmem_limit_bytes=VMEM_LIMIT),
        name="post",
    )(h, o, hc, ys, ctail, w["gsb"], w["wdw"], w["bdw"], w["lng"], w["lnb"], w["gconf"],
      w["wout"], w["lmg"], w["lmb"], w["w1"], w["w2"], w["lfg"], w["lfb"])


def _layer_weights(l, w_in, w_conf_dw, b_conf_dw, ln_conf_g, ln_conf_b, w_short_dw, g_mix, w_out,
                   ln_mix_g, ln_mix_b, w_ff1, w_ff2, ln_ff_g, ln_ff_b):
    wi = w_in[l]
    q_scale = LOG2E * HEAD_DIM ** -0.5
    r = lambda a: a.reshape(1, -1)
    return dict(
        wq=(wi[:, :SB_WIDTH] * q_scale).astype(BF16),
        wkt=wi[:, SB_WIDTH:2 * SB_WIDTH].T.astype(BF16),
        wv=wi[:, 2 * SB_WIDTH:3 * SB_WIDTH].astype(BF16),
        wconf=wi[:, 3 * SB_WIDTH:3 * SB_WIDTH + 2 * CONF_WIDTH].astype(BF16),
        wsc=wi[:, 3 * SB_WIDTH + 2 * CONF_WIDTH:].astype(BF16),
        wdw=w_conf_dw[l], bdw=r(b_conf_dw[l]), lng=r(ln_conf_g[l]), lnb=r(ln_conf_b[l]),
        wsh=w_short_dw[l],
        gsb=r(g_mix[l, :SB_WIDTH]), gconf=r(g_mix[l, SB_WIDTH:SB_WIDTH + CONF_WIDTH]),
        gsc=r(g_mix[l, SB_WIDTH + CONF_WIDTH:]),
        wout=w_out[l].astype(BF16), lmg=r(ln_mix_g[l]), lmb=r(ln_mix_b[l]),
        w1=w_ff1[l].astype(BF16), w2=w_ff2[l].astype(BF16), lfg=r(ln_ff_g[l]), lfb=r(ln_ff_b[l]),
    )


def kernel(x, meta_tokens, ln_in_g, ln_in_b, w_in, w_conf_dw, b_conf_dw, ln_conf_g, ln_conf_b,
           w_short_dw, g_mix, w_out, ln_mix_g, ln_mix_b, w_ff1, w_ff2, ln_ff_g, ln_ff_b):
    bsz, seq, _ = x.shape
    ln_in_gb = (ln_in_g.reshape(1, -1), ln_in_b.reshape(1, -1))
    hm = jnp.zeros((1, META_ROWS, D_MODEL), F32).at[0, :N_META].set(meta_tokens.astype(F32))
    hx = x

    rows = lax.broadcasted_iota(jnp.int32, (KV_TILE, KV_TILE), 0)
    cols = lax.broadcasted_iota(jnp.int32, (KV_TILE, KV_TILE), 1)
    ntri = -(rows >= cols).astype(BF16)
    zero_ctail = jnp.zeros((1, CONF_HALO, CONF_WIDTH), F32)
    zero_stail = jnp.zeros((1, SC_HALO, SC_WIDTH), F32)

    for l in range(DEPTH):
        w = _layer_weights(l, w_in, w_conf_dw, b_conf_dw, ln_conf_g, ln_conf_b, w_short_dw, g_mix,
                           w_out, ln_mix_g, ln_mix_b, w_ff1, w_ff2, ln_ff_g, ln_ff_b)
        first = l == 0
        qm, ktm, vm, hcm, ysm, stail, *hm0 = _proj(hm, ln_in_gb, w, zero_stail, tm=META_ROWS,
                                                   tail_at=N_META, ln_in=first)
        q, kt, v, hc, ys, _, *hx0 = _proj(hx, ln_in_gb, w, stail, tm=512, tail_at=512, ln_in=first)
        if first:
            (hm,), (hx,) = hm0, hx0
        ctx = _ctx_operands(ktm, vm)
        om = _attn(qm, ktm, vm, ctx, ntri, n_sub=1, n_ctx=0)
        hm, ctail = _post(hm, om, hcm, ysm, zero_ctail, w, tm=META_ROWS, n_sub=1, tail_at=N_META)
        o = _attn(q, kt, v, ctx, ntri, n_sub=2, n_ctx=N_META)
        hx, _ = _post(hx, o, hc, ys, ctail, w, tm=512, n_sub=2, tail_at=512)
    return hx
```

```python
import functools
import math

import jax
import jax.numpy as jnp
from jax import lax
from jax.experimental import pallas as pl
from jax.experimental.pallas import tpu as pltpu

D_MODEL = 1024
DEPTH = 4
N_META = 16
SB_WIDTH = 512
HEAD_DIM = 64
CONF_WIDTH = 256
CONF_KERNEL = 31
SC_WIDTH = 256
SC_KERNEL = 3
D_FF = 4 * D_MODEL
ALPHA = (2.0 * DEPTH) ** 0.25
LN_EPS = 1e-5
RMS_EPS = 1e-6
LOG2E = math.log2(math.e)

LANES = 128
SUBLANES = 8
CONV_CHUNK = 64
SOFTPLUS2_CAP = 64.0
DEAD_LOG2 = -160.0
KV_TILE = 256
CONF_HALO = 32
SC_HALO = 8
META_ROWS = 256
VMEM_LIMIT = 56 * 1024 * 1024

BF16 = jnp.bfloat16
F32 = jnp.float32


def _dot(a, b):
    return jnp.dot(a, b, preferred_element_type=F32)


def _layer_norm(x, g, b):
    mu = jnp.mean(x, axis=-1, keepdims=True)
    xc = x - mu
    var = jnp.mean(xc * xc, axis=-1, keepdims=True)
    return xc * lax.rsqrt(var + LN_EPS) * g + b


def _rms_norm(x, g):
    return x * lax.rsqrt(jnp.mean(x * x, axis=-1, keepdims=True) + RMS_EPS) * g


def _const_spec(shape):
    zeros = (0,) * len(shape)
    return pl.BlockSpec(shape, lambda *_: zeros)


def _proj_kernel(h_ref, gin_ref, bin_ref, wq_ref, wkt_ref, wv_ref, wconf_ref, wsc_ref, wsh_ref,
                 gsc_ref, stail_ref, q_ref, kt_ref, v_ref, hc_ref, ys_ref, stail_out, *rest,
                 tm, tail_at, ln_in):
    i = pl.program_id(1)
    if ln_in:
        h0_ref, xs_ref, bg_ref = rest
    else:
        xs_ref, bg_ref = rest

    @pl.when(i == 0)
    def _():
        xs_ref[0:SC_HALO, :] = stail_ref[0]

    if ln_in:
        h = _layer_norm(h_ref[0], gin_ref[...], bin_ref[...])
        h0_ref[0] = h
    else:
        h = h_ref[0]
    hb = h.astype(BF16)
    cw, sw = CONF_WIDTH, SC_WIDTH

    hc_ref[0] = _dot(hb, wconf_ref[:, 0:cw]) * jax.nn.sigmoid(_dot(hb, wconf_ref[:, cw:2 * cw]))
    bg_ref[...] = _dot(hb, wsc_ref[:, 0:sw])
    xs_ref[SC_HALO:SC_HALO + tm, :] = (_dot(hb, wsc_ref[:, sw:2 * sw])
                                       * _dot(hb, wsc_ref[:, 2 * sw:3 * sw]))
    stail_out[0] = xs_ref[tail_at:tail_at + SC_HALO, :]

    def short_chunk(t0, rows):
        conv = None
        for k in range(SC_KERNEL):
            off = t0 + SC_HALO - (SC_KERNEL - 1) + k
            term = wsh_ref[k:k + 1, :] * xs_ref[off:off + rows, :]
            conv = term if conv is None else conv + term
        ys_ref[0, t0:t0 + rows, :] = _rms_norm(bg_ref[t0:t0 + rows, :] * conv,
                                               gsc_ref[...]).astype(BF16)

    def q_block(c):
        cols = slice(c * KV_TILE, (c + 1) * KV_TILE)
        q_ref[0, :, cols] = _dot(hb, wq_ref[:, cols]).astype(BF16)

    def v_block(c):
        cols = slice(c * KV_TILE, (c + 1) * KV_TILE)
        v_ref[0, :, cols] = _dot(hb, wv_ref[:, cols]).astype(BF16)

    def kt_block(c):
        rows = slice(c * KV_TILE, (c + 1) * KV_TILE)
        kt_ref[0, c] = lax.dot_general(wkt_ref[...], hb[rows], (((1,), (1,)), ((), ())),
                                       preferred_element_type=F32).astype(BF16)

    matmuls = ([functools.partial(q_block, c) for c in range(SB_WIDTH // KV_TILE)]
               + [functools.partial(kt_block, c) for c in range(tm // KV_TILE)]
               + [functools.partial(v_block, c) for c in range(SB_WIDTH // KV_TILE)])
    vpu_work = [functools.partial(short_chunk, t0, 2 * CONV_CHUNK) for t0 in range(0, tm, 2 * CONV_CHUNK)]
    n_m, n_v = len(matmuls), len(vpu_work)
    for k in range(n_m):
        matmuls[k]()
        for job in vpu_work[k * n_v // n_m:(k + 1) * n_v // n_m]:
            job()

    xs_ref[0:SC_HALO, :] = xs_ref[tm:tm + SC_HALO, :]


def _proj(h, ln_in_gb, w, stail, *, tm, tail_at, ln_in):
    bsz, length, _ = h.shape
    n_kv = length // KV_TILE
    row = lambda width: pl.BlockSpec((1, tm, width), lambda b, i: (b, i, 0))
    in_specs = [
        row(D_MODEL), _const_spec((1, D_MODEL)), _const_spec((1, D_MODEL)),
        _const_spec((D_MODEL, SB_WIDTH)), _const_spec((SB_WIDTH, D_MODEL)),
        _const_spec((D_MODEL, SB_WIDTH)), _const_spec((D_MODEL, 2 * CONF_WIDTH)),
        _const_spec((D_MODEL, 3 * SC_WIDTH)),
        _const_spec((SC_KERNEL, SC_WIDTH)), _const_spec((1, SC_WIDTH)),
        _const_spec((1, SC_HALO, SC_WIDTH)),
    ]
    out_specs = [
        row(SB_WIDTH),
        pl.BlockSpec((1, tm // KV_TILE, SB_WIDTH, KV_TILE), lambda b, i: (b, i, 0, 0)),
        row(SB_WIDTH), row(CONF_WIDTH), row(SC_WIDTH),
        pl.BlockSpec((1, SC_HALO, SC_WIDTH), lambda b, i: (b, 0, 0)),
    ]
    out_shape = [
        jax.ShapeDtypeStruct((bsz, length, SB_WIDTH), BF16),
        jax.ShapeDtypeStruct((bsz, n_kv, SB_WIDTH, KV_TILE), BF16),
        jax.ShapeDtypeStruct((bsz, length, SB_WIDTH), BF16),
        jax.ShapeDtypeStruct((bsz, length, CONF_WIDTH), F32),
        jax.ShapeDtypeStruct((bsz, length, SC_WIDTH), BF16),
        jax.ShapeDtypeStruct((bsz, SC_HALO, SC_WIDTH), F32),
    ]
    if ln_in:
        out_specs.append(row(D_MODEL))
        out_shape.append(jax.ShapeDtypeStruct((bsz, length, D_MODEL), F32))
    return pl.pallas_call(
        functools.partial(_proj_kernel, tm=tm, tail_at=tail_at, ln_in=ln_in),
        grid=(bsz, length // tm),
        in_specs=in_specs, out_specs=out_specs, out_shape=out_shape,
        scratch_shapes=[pltpu.VMEM((tm + SC_HALO, SC_WIDTH), F32),
                        pltpu.VMEM((tm, SC_WIDTH), F32)],
        compiler_params=pltpu.CompilerParams(dimension_semantics=("arbitrary", "arbitrary"),
                                             vmem_limit_bytes=VMEM_LIMIT),
        name="proj",
    )(h, *ln_in_gb, w["wq"], w["wkt"], w["wv"], w["wconf"], w["wsc"], w["wsh"], w["gsc"], stail)


def _softplus2(z):
    return jnp.maximum(z, jnp.log2(1.0 + jnp.exp2(jnp.minimum(z, SOFTPLUS2_CAP))))


def _sb_section(chains, qh_ref, ntri, update_r, z_ref, lk_ref, a_ref, acc_ref, r_ref):
    n = len(chains)
    zs, betweens, pvs = {}, {}, {}
    for step in range(n + 3):
        if step < n:
            c, load_kt = chains[step][:2]
            zs[step] = _dot(qh_ref[c], load_kt())
        if 0 <= step - 1 < n:
            c, _, _, mask, _ = chains[step - 1]
            z = zs.pop(step - 1)
            width = z.shape[1]
            sp = _softplus2(z)
            if mask is not None:
                sp = jnp.where(mask, sp, 0.0)
            z_ref[c, :, 0:width] = z
            lk_ref[c, :, 0:width] = sp.astype(BF16)
            betweens[step - 1] = _dot(lk_ref[c, :, 0:width], ntri)
        if 0 <= step - 2 < n:
            c, _, load_vt, mask, first = chains[step - 2]
            between = betweens.pop(step - 2)
            log_a = z_ref[c, :, 0:width] + between
            if not first:
                log_a = log_a + jnp.concatenate([r_ref[c]] * (width // LANES), axis=1)
            a = jnp.exp2(log_a)
            if mask is not None:
                a = jnp.where(mask, a, 0.0)
            a_ref[c, :, 0:width] = a.astype(BF16)
            pvs[step - 2] = _dot(a_ref[c, :, 0:width], load_vt())
            if update_r:
                total = jnp.broadcast_to(between[:, 0:1], (between.shape[0], LANES))
                r_ref[c] = total if first else r_ref[c] + total
        if 0 <= step - 3 < n:
            c, first = chains[step - 3][0], chains[step - 3][4]
            pv = pvs.pop(step - 3)
            acc_ref[c] = pv if first else acc_ref[c] + pv


def _attn_kernel(q_ref, kt_ref, v_ref, kctx_ref, vctx_ref, ntri_ctx_ref, ntri_ref, o_ref,
                 qh_ref, z_ref, lk_ref, a_ref, acc_ref, r_ref, *, n_sub, n_ctx, n_tiles):
    i = pl.program_id(1)
    n_pairs = SB_WIDTH // LANES
    n_heads = 2 * n_pairs
    lane = lax.broadcasted_iota(jnp.int32, (KV_TILE, LANES), 1)
    for r in range(n_sub):
        for p in range(n_pairs):
            q2 = q_ref[0, r * KV_TILE:(r + 1) * KV_TILE, p * LANES:(p + 1) * LANES]
            zero = jnp.zeros_like(q2)
            qh_ref[r * n_heads + 2 * p] = jnp.where(lane < HEAD_DIM, q2, zero)
            qh_ref[r * n_heads + 2 * p + 1] = jnp.where(lane >= HEAD_DIM, q2, zero)
    scratch = (z_ref, lk_ref, a_ref, acc_ref, r_ref)

    def chains_for(rows, tile_of_row, mask_of_row, first):
        out = []
        for r in rows:
            j = tile_of_row(r)
            start = pl.multiple_of(j * KV_TILE, KV_TILE)
            for h in range(n_heads):
                lanes = slice((h // 2) * LANES, (h // 2 + 1) * LANES)
                out.append((r * n_heads + h,
                            functools.partial(lambda j, lanes: kt_ref[0, j, lanes, :], j, lanes),
                            functools.partial(lambda start, lanes: v_ref[0, pl.ds(start, KV_TILE), lanes],
                                              start, lanes),
                            mask_of_row(r), first))
        return out

    row = lax.broadcasted_iota(jnp.int32, (KV_TILE, KV_TILE), 0)
    col = lax.broadcasted_iota(jnp.int32, (KV_TILE, KV_TILE), 1)
    causal = col < row
    all_rows = range(n_sub)
    has_prev = [r for r in all_rows if n_tiles > 1 or r > 0]
    _sb_section(chains_for(all_rows, lambda r: n_sub * i + r, lambda r: causal, True)
                + chains_for(has_prev, lambda r: jnp.maximum(n_sub * i + r - 1, 0),
                             lambda r: None if r > 0 else jnp.broadcast_to(i > 0, causal.shape), False),
                qh_ref, ntri_ref[...], True, *scratch)

    still_alive = []
    for r in all_rows:
        def alive(r=r):
            return jnp.max(r_ref[r * n_heads:(r + 1) * n_heads]) > DEAD_LOG2

        def cond(carry):
            j, live = carry
            return jnp.logical_and(j >= 0, live)

        def body(carry, r=r, alive=alive):
            j, _ = carry
            _sb_section(chains_for([r], lambda _: j, lambda _: None, False), qh_ref, ntri_ref[...],
                        True, *scratch)
            return j - 1, alive()

        still_alive.append(lax.while_loop(cond, body, (n_sub * i + r - 2, alive()))[1])

    for r in all_rows:
        rows_r = slice(r * KV_TILE, (r + 1) * KV_TILE)
        o_ref[0, rows_r, :] = jnp.concatenate(
            [jnp.where(lane < HEAD_DIM, acc_ref[r * n_heads + 2 * p], acc_ref[r * n_heads + 2 * p + 1])
             for p in range(n_pairs)], axis=1)
        if n_ctx:
            @pl.when(still_alive[r])
            def _(r=r, rows_r=rows_r):
                z = _dot(q_ref[0, rows_r, :], kctx_ref[...])
                sp = _softplus2(z)
                log_a = z + _dot(sp.astype(BF16), ntri_ctx_ref[...])
                for h in range(n_heads):
                    log_a = log_a + jnp.where(lane // n_ctx == h, r_ref[r * n_heads + h], 0.0)
                o_ref[0, rows_r, :] += _dot(jnp.exp2(log_a).astype(BF16), vctx_ref[...])


def _ctx_operands(ktm, vm):
    n_heads = SB_WIDTH // HEAD_DIM
    assert n_heads * N_META == LANES
    kt_ctx = ktm[0, 0, :, :N_META]
    v_ctx = vm[0, :N_META, :]
    head_of_dim = lax.broadcasted_iota(jnp.int32, (SB_WIDTH, LANES), 0) // HEAD_DIM
    head_of_col = lax.broadcasted_iota(jnp.int32, (SB_WIDTH, LANES), 1) // N_META
    same = head_of_dim == head_of_col
    kctx = jnp.where(same, jnp.tile(kt_ctx, (1, n_heads)), 0).astype(BF16)
    vctx = jnp.where(same.T, jnp.tile(v_ctx, (n_heads, 1)), 0).astype(BF16)
    j = lax.broadcasted_iota(jnp.int32, (LANES, LANES), 0)
    s = lax.broadcasted_iota(jnp.int32, (LANES, LANES), 1)
    ntri_ctx = -((j >= s) & (j // N_META == s // N_META)).astype(BF16)
    return kctx, vctx, ntri_ctx


def _attn(q, kt, v, ctx, ntri, *, n_sub, n_ctx):
    bsz, length, _ = q.shape
    tq = n_sub * KV_TILE
    n_kv = length // KV_TILE
    n_chains = n_sub * SB_WIDTH // HEAD_DIM
    return pl.pallas_call(
        functools.partial(_attn_kernel, n_sub=n_sub, n_ctx=n_ctx, n_tiles=length // tq),
        grid=(bsz, length // tq),
        in_specs=[
            pl.BlockSpec((1, tq, SB_WIDTH), lambda b, i: (b, i, 0)),
            pl.BlockSpec((1, n_kv, SB_WIDTH, KV_TILE), lambda b, i: (b, 0, 0, 0)),
            pl.BlockSpec((1, length, SB_WIDTH), lambda b, i: (b, 0, 0)),
            _const_spec((SB_WIDTH, LANES)), _const_spec((LANES, SB_WIDTH)),
            _const_spec((LANES, LANES)), _const_spec((KV_TILE, KV_TILE)),
        ],
        out_specs=pl.BlockSpec((1, tq, SB_WIDTH), lambda b, i: (b, i, 0)),
        out_shape=jax.ShapeDtypeStruct((bsz, length, SB_WIDTH), F32),
        scratch_shapes=[pltpu.VMEM((n_chains, KV_TILE, LANES), BF16),
                        pltpu.VMEM((n_chains, KV_TILE, KV_TILE), F32),
                        pltpu.VMEM((n_chains, KV_TILE, KV_TILE), BF16),
                        pltpu.VMEM((n_chains, KV_TILE, KV_TILE), BF16),
                        pltpu.VMEM((n_chains, KV_TILE, LANES), F32),
                        pltpu.VMEM((n_chains, KV_TILE, LANES), F32)],
        compiler_params=pltpu.CompilerParams(dimension_semantics=("arbitrary", "arbitrary"),
                                             vmem_limit_bytes=VMEM_LIMIT),
        name="sb_attn",
    )(q, kt, v, *ctx, ntri)


def _post_kernel(h_ref, o_ref, hc_ref, ys_ref, ctail_ref, gsb_ref, wdw_ref, bdw_ref, lng_ref, lnb_ref,
                 gconf_ref, wout_ref, lmg_ref, lmb_ref, w1_ref, w2_ref, lfg_ref, lfb_ref,
                 out_ref, ctail_out, xc_ref, zs_ref, *, n_sub, tail_at):
    i = pl.program_id(1)
    tm = h_ref.shape[1]
    rows = tm // n_sub
    subs = [slice(s * rows, (s + 1) * rows) for s in range(n_sub)]
    sb, cw = SB_WIDTH, CONF_WIDTH

    @pl.when(i == 0)
    def _():
        xc_ref[0:CONF_HALO, :] = ctail_ref[0]

    xc_ref[CONF_HALO:CONF_HALO + tm, :] = hc_ref[0]
    ctail_out[0] = xc_ref[tail_at:tail_at + CONF_HALO, :]
    z_rows = tm + CONF_HALO - SUBLANES
    for r in range(SUBLANES):
        zs_ref[r] = xc_ref[SUBLANES - r:SUBLANES - r + z_rows, :]

    def conf_chunk(t0, n):
        acc = jnp.broadcast_to(bdw_ref[...], (n, cw))
        for d in range(CONF_KERNEL):
            a, r = divmod(d, SUBLANES)
            off = t0 + CONF_HALO - SUBLANES - SUBLANES * a
            k = CONF_KERNEL - 1 - d
            acc = acc + wdw_ref[k:k + 1, :] * zs_ref[r, off:off + n, :]
        y = _layer_norm(acc, lng_ref[...], lnb_ref[...])
        y = y * jax.nn.sigmoid(y)
        return _rms_norm(y, gconf_ref[...]).astype(BF16)

    def mix(r):
        ysb = _rms_norm(o_ref[0, r, :], gsb_ref[...]).astype(BF16)
        part = _dot(ysb, wout_ref[0:sb, :]) + _dot(ys_ref[0, r, :], wout_ref[sb + cw:, :])
        yc = jnp.concatenate([conf_chunk(t0, CONV_CHUNK)
                              for t0 in range(r.start, r.stop, CONV_CHUNK)], axis=0)
        mixed = part + _dot(yc, wout_ref[sb:sb + cw, :])
        return _layer_norm(ALPHA * h_ref[0, r, :] + mixed, lmg_ref[...], lmb_ref[...])

    def ff_in(h1):
        u = jnp.maximum(_dot(h1.astype(BF16), w1_ref[...]), 0.0)
        return (u * u).astype(BF16)

    h1s = [mix(r) for r in subs]
    u2s = [ff_in(h1) for h1 in h1s]
    for r, h1, u2 in zip(subs, h1s, u2s):
        out_ref[0, r, :] = _layer_norm(ALPHA * h1 + _dot(u2, w2_ref[...]), lfg_ref[...], lfb_ref[...])

    xc_ref[0:CONF_HALO, :] = xc_ref[tm:tm + CONF_HALO, :]


def _post(h, o, hc, ys, ctail, w, *, tm, n_sub, tail_at):
    bsz, length, _ = h.shape
    row = lambda width: pl.BlockSpec((1, tm, width), lambda b, i: (b, i, 0))
    single = lambda shape: pl.BlockSpec(shape, lambda *_: (0,) * len(shape),
                                        pipeline_mode=pl.Buffered(1))
    tail_spec = lambda index_map: pl.BlockSpec((1, CONF_HALO, CONF_WIDTH), index_map)
    return pl.pallas_call(
        functools.partial(_post_kernel, n_sub=n_sub, tail_at=tail_at),
        grid=(bsz, length // tm),
        in_specs=[row(D_MODEL), row(SB_WIDTH), row(CONF_WIDTH), row(SC_WIDTH),
                  tail_spec(lambda b, i: (0, 0, 0)), _const_spec((1, SB_WIDTH)),
                  _const_spec((CONF_KERNEL, CONF_WIDTH)), _const_spec((1, CONF_WIDTH)),
                  _const_spec((1, CONF_WIDTH)), _const_spec((1, CONF_WIDTH)),
                  _const_spec((1, CONF_WIDTH)), single((D_MODEL, D_MODEL)),
                  _const_spec((1, D_MODEL)), _const_spec((1, D_MODEL)),
                  single((D_MODEL, D_FF)), single((D_FF, D_MODEL)),
                  _const_spec((1, D_MODEL)), _const_spec((1, D_MODEL))],
        out_specs=[row(D_MODEL), tail_spec(lambda b, i: (b, 0, 0))],
        out_shape=[jax.ShapeDtypeStruct((bsz, length, D_MODEL), F32),
                   jax.ShapeDtypeStruct((bsz, CONF_HALO, CONF_WIDTH), F32)],
        scratch_shapes=[pltpu.VMEM((tm + CONF_HALO, CONF_WIDTH), F32),
                        pltpu.VMEM((SUBLANES, tm + CONF_HALO - SUBLANES, CONF_WIDTH), F32)],
        compiler_params=pltpu.CompilerParams(dimension_semantics=("arbitrary", "arbitrary"),
                                             vmem_limit_bytes=VMEM_LIMIT),
        name="post",
    )(h, o, hc, ys, ctail, w["gsb"], w["wdw"], w["bdw"], w["lng"], w["lnb"], w["gconf"],
      w["wout"], w["lmg"], w["lmb"], w["w1"], w["w2"], w["lfg"], w["lfb"])


def _layer_weights(l, w_in, w_conf_dw, b_conf_dw, ln_conf_g, ln_conf_b, w_short_dw, g_mix, w_out,
                   ln_mix_g, ln_mix_b, w_ff1, w_ff2, ln_ff_g, ln_ff_b):
    wi = w_in[l]
    q_scale = LOG2E * HEAD_DIM ** -0.5
    r = lambda a: a.reshape(1, -1)
    return dict(
        wq=(wi[:, :SB_WIDTH] * q_scale).astype(BF16),
        wkt=wi[:, SB_WIDTH:2 * SB_WIDTH].T.astype(BF16),
        wv=wi[:, 2 * SB_WIDTH:3 * SB_WIDTH].astype(BF16),
        wconf=wi[:, 3 * SB_WIDTH:3 * SB_WIDTH + 2 * CONF_WIDTH].astype(BF16),
        wsc=wi[:, 3 * SB_WIDTH + 2 * CONF_WIDTH:].astype(BF16),
        wdw=w_conf_dw[l], bdw=r(b_conf_dw[l]), lng=r(ln_conf_g[l]), lnb=r(ln_conf_b[l]),
        wsh=w_short_dw[l],
        gsb=r(g_mix[l, :SB_WIDTH]), gconf=r(g_mix[l, SB_WIDTH:SB_WIDTH + CONF_WIDTH]),
        gsc=r(g_mix[l, SB_WIDTH + CONF_WIDTH:]),
        wout=w_out[l].astype(BF16), lmg=r(ln_mix_g[l]), lmb=r(ln_mix_b[l]),
        w1=w_ff1[l].astype(BF16), w2=w_ff2[l].astype(BF16), lfg=r(ln_ff_g[l]), lfb=r(ln_ff_b[l]),
    )


def kernel(x, meta_tokens, ln_in_g, ln_in_b, w_in, w_conf_dw, b_conf_dw, ln_conf_g, ln_conf_b,
           w_short_dw, g_mix, w_out, ln_mix_g, ln_mix_b, w_ff1, w_ff2, ln_ff_g, ln_ff_b):
    bsz, seq, _ = x.shape
    ln_in_gb = (ln_in_g.reshape(1, -1), ln_in_b.reshape(1, -1))
    hm = jnp.zeros((1, META_ROWS, D_MODEL), F32).at[0, :N_META].set(meta_tokens.astype(F32))
    hx = x

    rows = lax.broadcasted_iota(jnp.int32, (KV_TILE, KV_TILE), 0)
    cols = lax.broadcasted_iota(jnp.int32, (KV_TILE, KV_TILE), 1)
    ntri = -(rows >= cols).astype(BF16)
    zero_ctail = jnp.zeros((1, CONF_HALO, CONF_WIDTH), F32)
    zero_stail = jnp.zeros((1, SC_HALO, SC_WIDTH), F32)

    for l in range(DEPTH):
        w = _layer_weights(l, w_in, w_conf_dw, b_conf_dw, ln_conf_g, ln_conf_b, w_short_dw, g_mix,
                           w_out, ln_mix_g, ln_mix_b, w_ff1, w_ff2, ln_ff_g, ln_ff_b)
        first = l == 0
        qm, ktm, vm, hcm, ysm, stail, *hm0 = _proj(hm, ln_in_gb, w, zero_stail, tm=META_ROWS,
                                                   tail_at=N_META, ln_in=first)
        q, kt, v, hc, ys, _, *hx0 = _proj(hx, ln_in_gb, w, stail, tm=512, tail_at=512, ln_in=first)
        if first:
            (hm,), (hx,) = hm0, hx0
        ctx = _ctx_operands(ktm, vm)
        om = _attn(qm, ktm, vm, ctx, ntri, n_sub=1, n_ctx=0)
        hm, ctail = _post(hm, om, hcm, ysm, zero_ctail, w, tm=META_ROWS, n_sub=1, tail_at=N_META)
        o = _attn(q, kt, v, ctx, ntri, n_sub=2, n_ctx=N_META)
        hx, _ = _post(hx, o, hc, ys, ctail, w, tm=512, n_sub=2, tail_at=512)
    return hx
```

```python
import functools
import math

import jax
import jax.numpy as jnp
from jax import lax
from jax.experimental import pallas as pl
from jax.experimental.pallas import tpu as pltpu

D_MODEL = 1024
DEPTH = 4
N_META = 16
SB_WIDTH = 512
HEAD_DIM = 64
CONF_WIDTH = 256
CONF_KERNEL = 31
SC_WIDTH = 256
SC_KERNEL = 3
D_FF = 4 * D_MODEL
ALPHA = (2.0 * DEPTH) ** 0.25
LN_EPS = 1e-5
RMS_EPS = 1e-6
LOG2E = math.log2(math.e)

LANES = 128
SUBLANES = 8
CONV_CHUNK = 64
SOFTPLUS2_CAP = 64.0
DEAD_LOG2 = -160.0
KV_TILE = 256
CONF_HALO = 32
SC_HALO = 8
META_ROWS = 256
ROW_TILE = 512
POST_SUBTILES = 2
Q_GROUPS = 2
VMEM_LIMIT = 56 * 1024 * 1024

BF16 = jnp.bfloat16
F32 = jnp.float32


def _dot(a, b):
    return jnp.dot(a, b, preferred_element_type=F32)


def _layer_norm(x, g, b):
    mu = jnp.mean(x, axis=-1, keepdims=True)
    xc = x - mu
    var = jnp.mean(xc * xc, axis=-1, keepdims=True)
    return xc * lax.rsqrt(var + LN_EPS) * g + b


def _rms_norm(x, g):
    return x * lax.rsqrt(jnp.mean(x * x, axis=-1, keepdims=True) + RMS_EPS) * g


def _const_spec(shape):
    zeros = (0,) * len(shape)
    return pl.BlockSpec(shape, lambda *_: zeros)


def _proj_kernel(h_ref, gin_ref, bin_ref, wq_ref, wkt_ref, wv_ref, wconf_ref, wsc_ref, wsh_ref,
                 gsc_ref, stail_ref, q_ref, kt_ref, v_ref, hc_ref, ys_ref, stail_out, *rest,
                 tm, tail_at, ln_in):
    i = pl.program_id(1)
    if ln_in:
        h0_ref, xs_ref, bg_ref = rest
    else:
        xs_ref, bg_ref = rest

    @pl.when(i == 0)
    def _():
        xs_ref[0:SC_HALO, :] = stail_ref[0]

    if ln_in:
        h = _layer_norm(h_ref[0], gin_ref[...], bin_ref[...])
        h0_ref[0] = h
    else:
        h = h_ref[0]
    hb = h.astype(BF16)
    cw, sw = CONF_WIDTH, SC_WIDTH

    hc_ref[0] = _dot(hb, wconf_ref[:, 0:cw]) * jax.nn.sigmoid(_dot(hb, wconf_ref[:, cw:2 * cw]))
    bg_ref[...] = _dot(hb, wsc_ref[:, 0:sw])
    xs_ref[SC_HALO:SC_HALO + tm, :] = (_dot(hb, wsc_ref[:, sw:2 * sw])
                                       * _dot(hb, wsc_ref[:, 2 * sw:3 * sw]))
    stail_out[0] = xs_ref[tail_at:tail_at + SC_HALO, :]

    def short_chunk(t0, rows):
        conv = None
        for k in range(SC_KERNEL):
            off = t0 + SC_HALO - (SC_KERNEL - 1) + k
            term = wsh_ref[k:k + 1, :] * xs_ref[off:off + rows, :]
            conv = term if conv is None else conv + term
        ys_ref[0, t0:t0 + rows, :] = _rms_norm(bg_ref[t0:t0 + rows, :] * conv,
                                               gsc_ref[...]).astype(BF16)

    def q_block(c):
        cols = slice(c * KV_TILE, (c + 1) * KV_TILE)
        q_ref[0, :, cols] = _dot(hb, wq_ref[:, cols]).astype(BF16)

    def v_block(c):
        cols = slice(c * KV_TILE, (c + 1) * KV_TILE)
        v_ref[0, :, cols] = _dot(hb, wv_ref[:, cols]).astype(BF16)

    def kt_block(c):
        rows = slice(c * KV_TILE, (c + 1) * KV_TILE)
        kt_ref[0, c] = lax.dot_general(wkt_ref[...], hb[rows], (((1,), (1,)), ((), ())),
                                       preferred_element_type=F32).astype(BF16)

    matmuls = ([functools.partial(q_block, c) for c in range(SB_WIDTH // KV_TILE)]
               + [functools.partial(kt_block, c) for c in range(tm // KV_TILE)]
               + [functools.partial(v_block, c) for c in range(SB_WIDTH // KV_TILE)])
    vpu_work = [functools.partial(short_chunk, t0, 2 * CONV_CHUNK) for t0 in range(0, tm, 2 * CONV_CHUNK)]
    n_m, n_v = len(matmuls), len(vpu_work)
    for k in range(n_m):
        matmuls[k]()
        for job in vpu_work[k * n_v // n_m:(k + 1) * n_v // n_m]:
            job()

    xs_ref[0:SC_HALO, :] = xs_ref[tm:tm + SC_HALO, :]


def _proj(h, ln_in_gb, w, stail, *, tm, tail_at, ln_in):
    bsz, length, _ = h.shape
    n_kv = length // KV_TILE
    row = lambda width: pl.BlockSpec((1, tm, width), lambda b, i: (b, i, 0))
    in_specs = [
        row(D_MODEL), _const_spec((1, D_MODEL)), _const_spec((1, D_MODEL)),
        _const_spec((D_MODEL, SB_WIDTH)), _const_spec((SB_WIDTH, D_MODEL)),
        _const_spec((D_MODEL, SB_WIDTH)), _const_spec((D_MODEL, 2 * CONF_WIDTH)),
        _const_spec((D_MODEL, 3 * SC_WIDTH)),
        _const_spec((SC_KERNEL, SC_WIDTH)), _const_spec((1, SC_WIDTH)),
        _const_spec((1, SC_HALO, SC_WIDTH)),
    ]
    out_specs = [
        row(SB_WIDTH),
        pl.BlockSpec((1, tm // KV_TILE, SB_WIDTH, KV_TILE), lambda b, i: (b, i, 0, 0)),
        row(SB_WIDTH), row(CONF_WIDTH), row(SC_WIDTH),
        pl.BlockSpec((1, SC_HALO, SC_WIDTH), lambda b, i: (b, 0, 0)),
    ]
    out_shape = [
        jax.ShapeDtypeStruct((bsz, length, SB_WIDTH), BF16),
        jax.ShapeDtypeStruct((bsz, n_kv, SB_WIDTH, KV_TILE), BF16),
        jax.ShapeDtypeStruct((bsz, length, SB_WIDTH), BF16),
        jax.ShapeDtypeStruct((bsz, length, CONF_WIDTH), F32),
        jax.ShapeDtypeStruct((bsz, length, SC_WIDTH), BF16),
        jax.ShapeDtypeStruct((bsz, SC_HALO, SC_WIDTH), F32),
    ]
    if ln_in:
        out_specs.append(row(D_MODEL))
        out_shape.append(jax.ShapeDtypeStruct((bsz, length, D_MODEL), F32))
    return pl.pallas_call(
        functools.partial(_proj_kernel, tm=tm, tail_at=tail_at, ln_in=ln_in),
        grid=(bsz, length // tm),
        in_specs=in_specs, out_specs=out_specs, out_shape=out_shape,
        scratch_shapes=[pltpu.VMEM((tm + SC_HALO, SC_WIDTH), F32),
                        pltpu.VMEM((tm, SC_WIDTH), F32)],
        compiler_params=pltpu.CompilerParams(dimension_semantics=("arbitrary", "arbitrary"),
                                             vmem_limit_bytes=VMEM_LIMIT),
        name="proj",
    )(h, *ln_in_gb, w["wq"], w["wkt"], w["wv"], w["wconf"], w["wsc"], w["wsh"], w["gsc"], stail)


def _softplus2(z):
    return jnp.maximum(z, jnp.log2(1.0 + jnp.exp2(jnp.minimum(z, SOFTPLUS2_CAP))))


def _sb_section(chains, qh_ref, ntri, update_r, z_ref, lk_ref, a_ref, acc_ref, r_ref):
    n = len(chains)
    zs, betweens, pvs = {}, {}, {}
    for step in range(n + 3):
        if step < n:
            c, load_kt = chains[step][:2]
            zs[step] = _dot(qh_ref[c], load_kt())
        if 0 <= step - 1 < n:
            c, _, _, mask, _ = chains[step - 1]
            z = zs.pop(step - 1)
            width = z.shape[1]
            sp = _softplus2(z)
            if mask is not None:
                sp = jnp.where(mask, sp, 0.0)
            z_ref[c, :, 0:width] = z
            lk_ref[c, :, 0:width] = sp.astype(BF16)
            betweens[step - 1] = _dot(lk_ref[c, :, 0:width], ntri)
        if 0 <= step - 2 < n:
            c, _, load_vt, mask, first = chains[step - 2]
            between = betweens.pop(step - 2)
            log_a = z_ref[c, :, 0:width] + between
            if not first:
                log_a = log_a + jnp.concatenate([r_ref[c]] * (width // LANES), axis=1)
            a = jnp.exp2(log_a)
            if mask is not None:
                a = jnp.where(mask, a, 0.0)
            a_ref[c, :, 0:width] = a.astype(BF16)
            pvs[step - 2] = _dot(a_ref[c, :, 0:width], load_vt())
            if update_r:
                total = jnp.broadcast_to(between[:, 0:1], (between.shape[0], LANES))
                r_ref[c] = total if first else r_ref[c] + total
        if 0 <= step - 3 < n:
            c, first = chains[step - 3][0], chains[step - 3][4]
            pv = pvs.pop(step - 3)
            acc_ref[c] = pv if first else acc_ref[c] + pv


def _attn_kernel(q_ref, kt_ref, v_ref, kctx_ref, vctx_ref, ntri_ctx_ref, ntri_ref, o_ref,
                 qh_ref, z_ref, lk_ref, a_ref, acc_ref, r_ref, *, n_sub, n_ctx, n_tiles):
    i = pl.program_id(1)
    n_pairs = SB_WIDTH // LANES
    n_heads = 2 * n_pairs
    lane = lax.broadcasted_iota(jnp.int32, (KV_TILE, LANES), 1)
    for r in range(n_sub):
        for p in range(n_pairs):
            q2 = q_ref[0, r * KV_TILE:(r + 1) * KV_TILE, p * LANES:(p + 1) * LANES]
            zero = jnp.zeros_like(q2)
            qh_ref[r * n_heads + 2 * p] = jnp.where(lane < HEAD_DIM, q2, zero)
            qh_ref[r * n_heads + 2 * p + 1] = jnp.where(lane >= HEAD_DIM, q2, zero)
    scratch = (z_ref, lk_ref, a_ref, acc_ref, r_ref)

    def chains_for(rows, tile_of_row, mask_of_row, first):
        out = []
        for r in rows:
            j = tile_of_row(r)
            start = pl.multiple_of(j * KV_TILE, KV_TILE)
            for h in range(n_heads):
                lanes = slice((h // 2) * LANES, (h // 2 + 1) * LANES)
                out.append((r * n_heads + h,
                            functools.partial(lambda j, lanes: kt_ref[0, j, lanes, :], j, lanes),
                            functools.partial(lambda start, lanes: v_ref[0, pl.ds(start, KV_TILE), lanes],
                                              start, lanes),
                            mask_of_row(r), first))
        return out

    row = lax.broadcasted_iota(jnp.int32, (KV_TILE, KV_TILE), 0)
    col = lax.broadcasted_iota(jnp.int32, (KV_TILE, KV_TILE), 1)
    causal = col < row
    all_rows = range(n_sub)
    has_prev = [r for r in all_rows if n_tiles > 1 or r > 0]
    _sb_section(chains_for(all_rows, lambda r: n_sub * i + r, lambda r: causal, True)
                + chains_for(has_prev, lambda r: jnp.maximum(n_sub * i + r - 1, 0),
                             lambda r: None if r > 0 else jnp.broadcast_to(i > 0, causal.shape), False),
                qh_ref, ntri_ref[...], True, *scratch)

    still_alive = []
    for r in all_rows:
        def alive(r=r):
            return jnp.max(r_ref[r * n_heads:(r + 1) * n_heads]) > DEAD_LOG2

        def cond(carry):
            j, live = carry
            return jnp.logical_and(j >= 0, live)

        def body(carry, r=r, alive=alive):
            j, _ = carry
            _sb_section(chains_for([r], lambda _: j, lambda _: None, False), qh_ref, ntri_ref[...],
                        True, *scratch)
            return j - 1, alive()

        still_alive.append(lax.while_loop(cond, body, (n_sub * i + r - 2, alive()))[1])

    for r in all_rows:
        rows_r = slice(r * KV_TILE, (r + 1) * KV_TILE)
        o_ref[0, rows_r, :] = jnp.concatenate(
            [jnp.where(lane < HEAD_DIM, acc_ref[r * n_heads + 2 * p], acc_ref[r * n_heads + 2 * p + 1])
             for p in range(n_pairs)], axis=1)
        if n_ctx:
            @pl.when(still_alive[r])
            def _(r=r, rows_r=rows_r):
                z = _dot(q_ref[0, rows_r, :], kctx_ref[...])
                sp = _softplus2(z)
                log_a = z + _dot(sp.astype(BF16), ntri_ctx_ref[...])
                for h in range(n_heads):
                    log_a = log_a + jnp.where(lane // n_ctx == h, r_ref[r * n_heads + h], 0.0)
                o_ref[0, rows_r, :] += _dot(jnp.exp2(log_a).astype(BF16), vctx_ref[...])


def _ctx_operands(ktm, vm):
    n_heads = SB_WIDTH // HEAD_DIM
    assert n_heads * N_META == LANES
    kt_ctx = ktm[0, 0, :, :N_META]
    v_ctx = vm[0, :N_META, :]
    head_of_dim = lax.broadcasted_iota(jnp.int32, (SB_WIDTH, LANES), 0) // HEAD_DIM
    head_of_col = lax.broadcasted_iota(jnp.int32, (SB_WIDTH, LANES), 1) // N_META
    same = head_of_dim == head_of_col
    kctx = jnp.where(same, jnp.tile(kt_ctx, (1, n_heads)), 0).astype(BF16)
    vctx = jnp.where(same.T, jnp.tile(v_ctx, (n_heads, 1)), 0).astype(BF16)
    j = lax.broadcasted_iota(jnp.int32, (LANES, LANES), 0)
    s = lax.broadcasted_iota(jnp.int32, (LANES, LANES), 1)
    ntri_ctx = -((j >= s) & (j // N_META == s // N_META)).astype(BF16)
    return kctx, vctx, ntri_ctx


def _attn(q, kt, v, ctx, ntri, *, n_sub, n_ctx):
    bsz, length, _ = q.shape
    tq = n_sub * KV_TILE
    n_kv = length // KV_TILE
    n_chains = n_sub * SB_WIDTH // HEAD_DIM
    return pl.pallas_call(
        functools.partial(_attn_kernel, n_sub=n_sub, n_ctx=n_ctx, n_tiles=length // tq),
        grid=(bsz, length // tq),
        in_specs=[
            pl.BlockSpec((1, tq, SB_WIDTH), lambda b, i: (b, i, 0)),
            pl.BlockSpec((1, n_kv, SB_WIDTH, KV_TILE), lambda b, i: (b, 0, 0, 0)),
            pl.BlockSpec((1, length, SB_WIDTH), lambda b, i: (b, 0, 0)),
            _const_spec((SB_WIDTH, LANES)), _const_spec((LANES, SB_WIDTH)),
            _const_spec((LANES, LANES)), _const_spec((KV_TILE, KV_TILE)),
        ],
        out_specs=pl.BlockSpec((1, tq, SB_WIDTH), lambda b, i: (b, i, 0)),
        out_shape=jax.ShapeDtypeStruct((bsz, length, SB_WIDTH), F32),
        scratch_shapes=[pltpu.VMEM((n_chains, KV_TILE, LANES), BF16),
                        pltpu.VMEM((n_chains, KV_TILE, KV_TILE), F32),
                        pltpu.VMEM((n_chains, KV_TILE, KV_TILE), BF16),
                        pltpu.VMEM((n_chains, KV_TILE, KV_TILE), BF16),
                        pltpu.VMEM((n_chains, KV_TILE, LANES), F32),
                        pltpu.VMEM((n_chains, KV_TILE, LANES), F32)],
        compiler_params=pltpu.CompilerParams(dimension_semantics=("arbitrary", "arbitrary"),
                                             vmem_limit_bytes=VMEM_LIMIT),
        name="sb_attn",
    )(q, kt, v, *ctx, ntri)


def _post_kernel(h_ref, o_ref, hc_ref, ys_ref, ctail_ref, gsb_ref, wdw_ref, bdw_ref, lng_ref, lnb_ref,
                 gconf_ref, wout_ref, lmg_ref, lmb_ref, w1_ref, w2_ref, lfg_ref, lfb_ref,
                 out_ref, ctail_out, xc_ref, zs_ref, *, n_sub, tail_at):
    i = pl.program_id(1)
    tm = h_ref.shape[1]
    rows = tm // n_sub
    subs = [slice(s * rows, (s + 1) * rows) for s in range(n_sub)]
    sb, cw = SB_WIDTH, CONF_WIDTH

    @pl.when(i == 0)
    def _():
        xc_ref[0:CONF_HALO, :] = ctail_ref[0]

    xc_ref[CONF_HALO:CONF_HALO + tm, :] = hc_ref[0]
    ctail_out[0] = xc_ref[tail_at:tail_at + CONF_HALO, :]
    z_rows = tm + CONF_HALO - SUBLANES
    for r in range(SUBLANES):
        zs_ref[r] = xc_ref[SUBLANES - r:SUBLANES - r + z_rows, :]

    def conf_chunk(t0, n):
        acc = jnp.broadcast_to(bdw_ref[...], (n, cw))
        for d in range(CONF_KERNEL):
            a, r = divmod(d, SUBLANES)
            off = t0 + CONF_HALO - SUBLANES - SUBLANES * a
            k = CONF_KERNEL - 1 - d
            acc = acc + wdw_ref[k:k + 1, :] * zs_ref[r, off:off + n, :]
        y = _layer_norm(acc, lng_ref[...], lnb_ref[...])
        y = y * jax.nn.sigmoid(y)
        return _rms_norm(y, gconf_ref[...]).astype(BF16)

    def mix(r):
        ysb = _rms_norm(o_ref[0, r, :], gsb_ref[...]).astype(BF16)
        part = _dot(ysb, wout_ref[0:sb, :]) + _dot(ys_ref[0, r, :], wout_ref[sb + cw:, :])
        yc = jnp.concatenate([conf_chunk(t0, CONV_CHUNK)
                              for t0 in range(r.start, r.stop, CONV_CHUNK)], axis=0)
        mixed = part + _dot(yc, wout_ref[sb:sb + cw, :])
        return _layer_norm(ALPHA * h_ref[0, r, :] + mixed, lmg_ref[...], lmb_ref[...])

    def ff_in(h1):
        u = jnp.maximum(_dot(h1.astype(BF16), w1_ref[...]), 0.0)
        return (u * u).astype(BF16)

    h1s = [mix(r) for r in subs]
    u2s = [ff_in(h1) for h1 in h1s]
    for r, h1, u2 in zip(subs, h1s, u2s):
        out_ref[0, r, :] = _layer_norm(ALPHA * h1 + _dot(u2, w2_ref[...]), lfg_ref[...], lfb_ref[...])

    xc_ref[0:CONF_HALO, :] = xc_ref[tm:tm + CONF_HALO, :]


def _post(h, o, hc, ys, ctail, w, *, tm, n_sub, tail_at):
    bsz, length, _ = h.shape
    row = lambda width: pl.BlockSpec((1, tm, width), lambda b, i: (b, i, 0))
    single = lambda shape: pl.BlockSpec(shape, lambda *_: (0,) * len(shape),
                                        pipeline_mode=pl.Buffered(1))
    tail_spec = lambda index_map: pl.BlockSpec((1, CONF_HALO, CONF_WIDTH), index_map)
    return pl.pallas_call(
        functools.partial(_post_kernel, n_sub=n_sub, tail_at=tail_at),
        grid=(bsz, length // tm),
        in_specs=[row(D_MODEL), row(SB_WIDTH), row(CONF_WIDTH), row(SC_WIDTH),
                  tail_spec(lambda b, i: (0, 0, 0)), _const_spec((1, SB_WIDTH)),
                  _const_spec((CONF_KERNEL, CONF_WIDTH)), _const_spec((1, CONF_WIDTH)),
                  _const_spec((1, CONF_WIDTH)), _const_spec((1, CONF_WIDTH)),
                  _const_spec((1, CONF_WIDTH)), single((D_MODEL, D_MODEL)),
                  _const_spec((1, D_MODEL)), _const_spec((1, D_MODEL)),
                  single((D_MODEL, D_FF)), single((D_FF, D_MODEL)),
                  _const_spec((1, D_MODEL)), _const_spec((1, D_MODEL))],
        out_specs=[row(D_MODEL), tail_spec(lambda b, i: (b, 0, 0))],
        out_shape=[jax.ShapeDtypeStruct((bsz, length, D_MODEL), F32),
                   jax.ShapeDtypeStruct((bsz, CONF_HALO, CONF_WIDTH), F32)],
        scratch_shapes=[pltpu.VMEM((tm + CONF_HALO, CONF_WIDTH), F32),
                        pltpu.VMEM((SUBLANES, tm + CONF_HALO - SUBLANES, CONF_WIDTH), F32)],
        compiler_params=pltpu.CompilerParams(dimension_semantics=("arbitrary", "arbitrary"),
                                             vmem_limit_bytes=VMEM_LIMIT),
        name="post",
    )(h, o, hc, ys, ctail, w["gsb"], w["wdw"], w["bdw"], w["lng"], w["lnb"], w["gconf"],
      w["wout"], w["lmg"], w["lmb"], w["w1"], w["w2"], w["lfg"], w["lfb"])


def _layer_weights(l, w_in, w_conf_dw, b_conf_dw, ln_conf_g, ln_conf_b, w_short_dw, g_mix, w_out,
                   ln_mix_g, ln_mix_b, w_ff1, w_ff2, ln_ff_g, ln_ff_b):
    wi = w_in[l]
    q_scale = LOG2E * HEAD_DIM ** -0.5
    r = lambda a: a.reshape(1, -1)
    return dict(
        wq=(wi[:, :SB_WIDTH] * q_scale).astype(BF16),
        wkt=wi[:, SB_WIDTH:2 * SB_WIDTH].T.astype(BF16),
        wv=wi[:, 2 * SB_WIDTH:3 * SB_WIDTH].astype(BF16),
        wconf=wi[:, 3 * SB_WIDTH:3 * SB_WIDTH + 2 * CONF_WIDTH].astype(BF16),
        wsc=wi[:, 3 * SB_WIDTH + 2 * CONF_WIDTH:].astype(BF16),
        wdw=w_conf_dw[l], bdw=r(b_conf_dw[l]), lng=r(ln_conf_g[l]), lnb=r(ln_conf_b[l]),
        wsh=w_short_dw[l],
        gsb=r(g_mix[l, :SB_WIDTH]), gconf=r(g_mix[l, SB_WIDTH:SB_WIDTH + CONF_WIDTH]),
        gsc=r(g_mix[l, SB_WIDTH + CONF_WIDTH:]),
        wout=w_out[l].astype(BF16), lmg=r(ln_mix_g[l]), lmb=r(ln_mix_b[l]),
        w1=w_ff1[l].astype(BF16), w2=w_ff2[l].astype(BF16), lfg=r(ln_ff_g[l]), lfb=r(ln_ff_b[l]),
    )


def kernel(x, meta_tokens, ln_in_g, ln_in_b, w_in, w_conf_dw, b_conf_dw, ln_conf_g, ln_conf_b,
           w_short_dw, g_mix, w_out, ln_mix_g, ln_mix_b, w_ff1, w_ff2, ln_ff_g, ln_ff_b):
    bsz, seq, _ = x.shape
    ln_in_gb = (ln_in_g.reshape(1, -1), ln_in_b.reshape(1, -1))
    hm = jnp.zeros((1, META_ROWS, D_MODEL), F32).at[0, :N_META].set(meta_tokens.astype(F32))
    hx = x

    rows = lax.broadcasted_iota(jnp.int32, (KV_TILE, KV_TILE), 0)
    cols = lax.broadcasted_iota(jnp.int32, (KV_TILE, KV_TILE), 1)
    ntri = -(rows >= cols).astype(BF16)
    zero_ctail = jnp.zeros((1, CONF_HALO, CONF_WIDTH), F32)
    zero_stail = jnp.zeros((1, SC_HALO, SC_WIDTH), F32)

    for l in range(DEPTH):
        w = _layer_weights(l, w_in, w_conf_dw, b_conf_dw, ln_conf_g, ln_conf_b, w_short_dw, g_mix,
                           w_out, ln_mix_g, ln_mix_b, w_ff1, w_ff2, ln_ff_g, ln_ff_b)
        first = l == 0
        qm, ktm, vm, hcm, ysm, stail, *hm0 = _proj(hm, ln_in_gb, w, zero_stail, tm=META_ROWS,
                                                   tail_at=N_META, ln_in=first)
        q, kt, v, hc, ys, _, *hx0 = _proj(hx, ln_in_gb, w, stail, tm=ROW_TILE, tail_at=ROW_TILE,
                                          ln_in=first)
        if first:
            (hm,), (hx,) = hm0, hx0
        ctx = _ctx_operands(ktm, vm)
        om = _attn(qm, ktm, vm, ctx, ntri, n_sub=1, n_ctx=0)
        hm, ctail = _post(hm, om, hcm, ysm, zero_ctail, w, tm=META_ROWS, n_sub=1, tail_at=N_META)
        o = _attn(q, kt, v, ctx, ntri, n_sub=Q_GROUPS, n_ctx=N_META)
        hx, _ = _post(hx, o, hc, ys, ctail, w, tm=ROW_TILE, n_sub=POST_SUBTILES, tail_at=ROW_TILE)
    return hx
```

```python
import functools
import math

import jax
import jax.numpy as jnp
from jax import lax
from jax.experimental import pallas as pl
from jax.experimental.pallas import tpu as pltpu

D_MODEL = 1024
DEPTH = 4
N_META = 16
SB_WIDTH = 512
HEAD_DIM = 64
CONF_WIDTH = 256
CONF_KERNEL = 31
SC_WIDTH = 256
SC_KERNEL = 3
D_FF = 4 * D_MODEL
ALPHA = (2.0 * DEPTH) ** 0.25
LN_EPS = 1e-5
RMS_EPS = 1e-6
LOG2E = math.log2(math.e)

LANES = 128
SUBLANES = 8
CONV_CHUNK = 64
SOFTPLUS2_CAP = 64.0
DEAD_LOG2 = -160.0
KV_TILE = 256
CONF_HALO = 32
SC_HALO = 8
META_ROWS = 256
ROW_TILE = 512
POST_SUBTILES = 2
Q_GROUPS = 2
VMEM_LIMIT = 56 * 1024 * 1024

BF16 = jnp.bfloat16
F32 = jnp.float32


def _dot(a, b):
    return jnp.dot(a, b, preferred_element_type=F32)


def _layer_norm(x, g, b):
    mu = jnp.mean(x, axis=-1, keepdims=True)
    xc = x - mu
    var = jnp.mean(xc * xc, axis=-1, keepdims=True)
    return xc * lax.rsqrt(var + LN_EPS) * g + b


def _rms_norm(x, g):
    return x * lax.rsqrt(jnp.mean(x * x, axis=-1, keepdims=True) + RMS_EPS) * g


def _const_spec(shape):
    zeros = (0,) * len(shape)
    return pl.BlockSpec(shape, lambda *_: zeros)


def _proj_kernel(h_ref, gin_ref, bin_ref, wq_ref, wkt_ref, wv_ref, wconf_ref, wsc_ref, wsh_ref,
                 gsc_ref, stail_ref, q_ref, kt_ref, v_ref, hc_ref, ys_ref, stail_out, *rest,
                 tm, tail_at, ln_in):
    i = pl.program_id(1)
    if ln_in:
        h0_ref, xs_ref, bg_ref = rest
    else:
        xs_ref, bg_ref = rest

    @pl.when(i == 0)
    def _():
        xs_ref[0:SC_HALO, :] = stail_ref[0]

    if ln_in:
        h = _layer_norm(h_ref[0], gin_ref[...], bin_ref[...])
        h0_ref[0] = h
    else:
        h = h_ref[0]
    hb = h.astype(BF16)
    cw, sw = CONF_WIDTH, SC_WIDTH

    hc_ref[0] = _dot(hb, wconf_ref[:, 0:cw]) * jax.nn.sigmoid(_dot(hb, wconf_ref[:, cw:2 * cw]))
    bg_ref[...] = _dot(hb, wsc_ref[:, 0:sw])
    xs_ref[SC_HALO:SC_HALO + tm, :] = (_dot(hb, wsc_ref[:, sw:2 * sw])
                                       * _dot(hb, wsc_ref[:, 2 * sw:3 * sw]))
    stail_out[0] = xs_ref[tail_at:tail_at + SC_HALO, :]

    def short_chunk(t0, rows):
        conv = None
        for k in range(SC_KERNEL):
            off = t0 + SC_HALO - (SC_KERNEL - 1) + k
            term = wsh_ref[k:k + 1, :] * xs_ref[off:off + rows, :]
            conv = term if conv is None else conv + term
        ys_ref[0, t0:t0 + rows, :] = _rms_norm(bg_ref[t0:t0 + rows, :] * conv,
                                               gsc_ref[...]).astype(BF16)

    def q_block(c):
        cols = slice(c * KV_TILE, (c + 1) * KV_TILE)
        q_ref[0, :, cols] = _dot(hb, wq_ref[:, cols]).astype(BF16)

    def v_block(c):
        cols = slice(c * KV_TILE, (c + 1) * KV_TILE)
        v_ref[0, :, cols] = _dot(hb, wv_ref[:, cols]).astype(BF16)

    def kt_block(c):
        rows = slice(c * KV_TILE, (c + 1) * KV_TILE)
        kt_ref[0, c] = lax.dot_general(wkt_ref[...], hb[rows], (((1,), (1,)), ((), ())),
                                       preferred_element_type=F32).astype(BF16)

    matmuls = ([functools.partial(q_block, c) for c in range(SB_WIDTH // KV_TILE)]
               + [functools.partial(kt_block, c) for c in range(tm // KV_TILE)]
               + [functools.partial(v_block, c) for c in range(SB_WIDTH // KV_TILE)])
    vpu_work = [functools.partial(short_chunk, t0, 2 * CONV_CHUNK) for t0 in range(0, tm, 2 * CONV_CHUNK)]
    n_m, n_v = len(matmuls), len(vpu_work)
    for k in range(n_m):
        matmuls[k]()
        for job in vpu_work[k * n_v // n_m:(k + 1) * n_v // n_m]:
            job()

    xs_ref[0:SC_HALO, :] = xs_ref[tm:tm + SC_HALO, :]


def _proj(h, ln_in_gb, w, stail, *, tm, tail_at, ln_in):
    bsz, length, _ = h.shape
    n_kv = length // KV_TILE
    row = lambda width: pl.BlockSpec((1, tm, width), lambda b, i: (b, i, 0))
    in_specs = [
        row(D_MODEL), _const_spec((1, D_MODEL)), _const_spec((1, D_MODEL)),
        _const_spec((D_MODEL, SB_WIDTH)), _const_spec((SB_WIDTH, D_MODEL)),
        _const_spec((D_MODEL, SB_WIDTH)), _const_spec((D_MODEL, 2 * CONF_WIDTH)),
        _const_spec((D_MODEL, 3 * SC_WIDTH)),
        _const_spec((SC_KERNEL, SC_WIDTH)), _const_spec((1, SC_WIDTH)),
        _const_spec((1, SC_HALO, SC_WIDTH)),
    ]
    out_specs = [
        row(SB_WIDTH),
        pl.BlockSpec((1, tm // KV_TILE, SB_WIDTH, KV_TILE), lambda b, i: (b, i, 0, 0)),
        row(SB_WIDTH), row(CONF_WIDTH), row(SC_WIDTH),
        pl.BlockSpec((1, SC_HALO, SC_WIDTH), lambda b, i: (b, 0, 0)),
    ]
    out_shape = [
        jax.ShapeDtypeStruct((bsz, length, SB_WIDTH), BF16),
        jax.ShapeDtypeStruct((bsz, n_kv, SB_WIDTH, KV_TILE), BF16),
        jax.ShapeDtypeStruct((bsz, length, SB_WIDTH), BF16),
        jax.ShapeDtypeStruct((bsz, length, CONF_WIDTH), F32),
        jax.ShapeDtypeStruct((bsz, length, SC_WIDTH), BF16),
        jax.ShapeDtypeStruct((bsz, SC_HALO, SC_WIDTH), F32),
    ]
    if ln_in:
        out_specs.append(row(D_MODEL))
        out_shape.append(jax.ShapeDtypeStruct((bsz, length, D_MODEL), F32))
    return pl.pallas_call(
        functools.partial(_proj_kernel, tm=tm, tail_at=tail_at, ln_in=ln_in),
        grid=(bsz, length // tm),
        in_specs=in_specs, out_specs=out_specs, out_shape=out_shape,
        scratch_shapes=[pltpu.VMEM((tm + SC_HALO, SC_WIDTH), F32),
                        pltpu.VMEM((tm, SC_WIDTH), F32)],
        compiler_params=pltpu.CompilerParams(dimension_semantics=("arbitrary", "arbitrary"),
                                             vmem_limit_bytes=VMEM_LIMIT),
        name="proj",
    )(h, *ln_in_gb, w["wq"], w["wkt"], w["wv"], w["wconf"], w["wsc"], w["wsh"], w["gsc"], stail)


def _softplus2(z):
    return jnp.maximum(z, jnp.log2(1.0 + jnp.exp2(jnp.minimum(z, SOFTPLUS2_CAP))))


def _sb_section(chains, qh_ref, ntri, update_r, z_ref, lk_ref, a_ref, acc_ref, r_ref):
    n = len(chains)
    zs, betweens, pvs = {}, {}, {}
    for step in range(n + 3):
        if step < n:
            c, load_kt = chains[step][:2]
            zs[step] = _dot(qh_ref[c], load_kt())
        if 0 <= step - 1 < n:
            c, _, _, mask, _ = chains[step - 1]
            z = zs.pop(step - 1)
            width = z.shape[1]
            sp = _softplus2(z)
            if mask is not None:
                sp = jnp.where(mask, sp, 0.0)
            z_ref[c, :, 0:width] = z
            lk_ref[c, :, 0:width] = sp.astype(BF16)
            betweens[step - 1] = _dot(lk_ref[c, :, 0:width], ntri)
        if 0 <= step - 2 < n:
            c, _, load_vt, mask, first = chains[step - 2]
            between = betweens.pop(step - 2)
            log_a = z_ref[c, :, 0:width] + between
            if not first:
                log_a = log_a + jnp.concatenate([r_ref[c]] * (width // LANES), axis=1)
            a = jnp.exp2(log_a)
            if mask is not None:
                a = jnp.where(mask, a, 0.0)
            a_ref[c, :, 0:width] = a.astype(BF16)
            pvs[step - 2] = _dot(a_ref[c, :, 0:width], load_vt())
            if update_r:
                total = jnp.broadcast_to(between[:, 0:1], (between.shape[0], LANES))
                r_ref[c] = total if first else r_ref[c] + total
        if 0 <= step - 3 < n:
            c, first = chains[step - 3][0], chains[step - 3][4]
            pv = pvs.pop(step - 3)
            acc_ref[c] = pv if first else acc_ref[c] + pv


def _attn_kernel(q_ref, kt_ref, v_ref, kctx_ref, vctx_ref, ntri_ctx_ref, ntri_ref, o_ref,
                 qh_ref, z_ref, lk_ref, a_ref, acc_ref, r_ref, *, n_sub, n_ctx, n_tiles):
    i = pl.program_id(1)
    n_pairs = SB_WIDTH // LANES
    n_heads = 2 * n_pairs
    lane = lax.broadcasted_iota(jnp.int32, (KV_TILE, LANES), 1)
    for r in range(n_sub):
        for p in range(n_pairs):
            q2 = q_ref[0, r * KV_TILE:(r + 1) * KV_TILE, p * LANES:(p + 1) * LANES]
            zero = jnp.zeros_like(q2)
            qh_ref[r * n_heads + 2 * p] = jnp.where(lane < HEAD_DIM, q2, zero)
            qh_ref[r * n_heads + 2 * p + 1] = jnp.where(lane >= HEAD_DIM, q2, zero)
    scratch = (z_ref, lk_ref, a_ref, acc_ref, r_ref)

    def chains_for(rows, tile_of_row, mask_of_row, first):
        out = []
        for r in rows:
            j = tile_of_row(r)
            start = pl.multiple_of(j * KV_TILE, KV_TILE)
            for h in range(n_heads):
                lanes = slice((h // 2) * LANES, (h // 2 + 1) * LANES)
                out.append((r * n_heads + h,
                            functools.partial(lambda j, lanes: kt_ref[0, j, lanes, :], j, lanes),
                            functools.partial(lambda start, lanes: v_ref[0, pl.ds(start, KV_TILE), lanes],
                                              start, lanes),
                            mask_of_row(r), first))
        return out

    row = lax.broadcasted_iota(jnp.int32, (KV_TILE, KV_TILE), 0)
    col = lax.broadcasted_iota(jnp.int32, (KV_TILE, KV_TILE), 1)
    causal = col < row
    all_rows = range(n_sub)
    has_prev = [r for r in all_rows if n_tiles > 1 or r > 0]
    _sb_section(chains_for(all_rows, lambda r: n_sub * i + r, lambda r: causal, True)
                + chains_for(has_prev, lambda r: jnp.maximum(n_sub * i + r - 1, 0),
                             lambda r: None if r > 0 else jnp.broadcast_to(i > 0, causal.shape), False),
                qh_ref, ntri_ref[...], True, *scratch)

    still_alive = []
    alive_now = [jnp.max(r_ref[r * n_heads:(r + 1) * n_heads]) > DEAD_LOG2 for r in all_rows]
    for r in all_rows:
        def alive(r=r):
            return jnp.max(r_ref[r * n_heads:(r + 1) * n_heads]) > DEAD_LOG2

        def cond(carry):
            j, live = carry
            return jnp.logical_and(j >= 0, live)

        def body(carry, r=r, alive=alive):
            j, _ = carry
            _sb_section(chains_for([r], lambda _: j, lambda _: None, False), qh_ref, ntri_ref[...],
                        True, *scratch)
            return j - 1, alive()

        still_alive.append(lax.while_loop(cond, body, (n_sub * i + r - 2, alive_now[r]))[1])

    for r in all_rows:
        rows_r = slice(r * KV_TILE, (r + 1) * KV_TILE)
        o_ref[0, rows_r, :] = jnp.concatenate(
            [jnp.where(lane < HEAD_DIM, acc_ref[r * n_heads + 2 * p], acc_ref[r * n_heads + 2 * p + 1])
             for p in range(n_pairs)], axis=1)
        if n_ctx:
            @pl.when(still_alive[r])
            def _(r=r, rows_r=rows_r):
                z = _dot(q_ref[0, rows_r, :], kctx_ref[...])
                sp = _softplus2(z)
                log_a = z + _dot(sp.astype(BF16), ntri_ctx_ref[...])
                for h in range(n_heads):
                    log_a = log_a + jnp.where(lane // n_ctx == h, r_ref[r * n_heads + h], 0.0)
                o_ref[0, rows_r, :] += _dot(jnp.exp2(log_a).astype(BF16), vctx_ref[...])


def _ctx_operands(ktm, vm):
    n_heads = SB_WIDTH // HEAD_DIM
    assert n_heads * N_META == LANES
    kt_ctx = ktm[0, 0, :, :N_META]
    v_ctx = vm[0, :N_META, :]
    head_of_dim = lax.broadcasted_iota(jnp.int32, (SB_WIDTH, LANES), 0) // HEAD_DIM
    head_of_col = lax.broadcasted_iota(jnp.int32, (SB_WIDTH, LANES), 1) // N_META
    same = head_of_dim == head_of_col
    kctx = jnp.where(same, jnp.tile(kt_ctx, (1, n_heads)), 0).astype(BF16)
    vctx = jnp.where(same.T, jnp.tile(v_ctx, (n_heads, 1)), 0).astype(BF16)
    j = lax.broadcasted_iota(jnp.int32, (LANES, LANES), 0)
    s = lax.broadcasted_iota(jnp.int32, (LANES, LANES), 1)
    ntri_ctx = -((j >= s) & (j // N_META == s // N_META)).astype(BF16)
    return kctx, vctx, ntri_ctx


def _attn(q, kt, v, ctx, ntri, *, n_sub, n_ctx):
    bsz, length, _ = q.shape
    tq = n_sub * KV_TILE
    n_kv = length // KV_TILE
    n_chains = n_sub * SB_WIDTH // HEAD_DIM
    return pl.pallas_call(
        functools.partial(_attn_kernel, n_sub=n_sub, n_ctx=n_ctx, n_tiles=length // tq),
        grid=(bsz, length // tq),
        in_specs=[
            pl.BlockSpec((1, tq, SB_WIDTH), lambda b, i: (b, i, 0)),
            pl.BlockSpec((1, n_kv, SB_WIDTH, KV_TILE), lambda b, i: (b, 0, 0, 0)),
            pl.BlockSpec((1, length, SB_WIDTH), lambda b, i: (b, 0, 0)),
            _const_spec((SB_WIDTH, LANES)), _const_spec((LANES, SB_WIDTH)),
            _const_spec((LANES, LANES)), _const_spec((KV_TILE, KV_TILE)),
        ],
        out_specs=pl.BlockSpec((1, tq, SB_WIDTH), lambda b, i: (b, i, 0)),
        out_shape=jax.ShapeDtypeStruct((bsz, length, SB_WIDTH), F32),
        scratch_shapes=[pltpu.VMEM((n_chains, KV_TILE, LANES), BF16),
                        pltpu.VMEM((n_chains, KV_TILE, KV_TILE), F32),
                        pltpu.VMEM((n_chains, KV_TILE, KV_TILE), BF16),
                        pltpu.VMEM((n_chains, KV_TILE, KV_TILE), BF16),
                        pltpu.VMEM((n_chains, KV_TILE, LANES), F32),
                        pltpu.VMEM((n_chains, KV_TILE, LANES), F32)],
        compiler_params=pltpu.CompilerParams(dimension_semantics=("arbitrary", "arbitrary"),
                                             vmem_limit_bytes=VMEM_LIMIT),
        name="sb_attn",
    )(q, kt, v, *ctx, ntri)


def _post_kernel(h_ref, o_ref, hc_ref, ys_ref, ctail_ref, gsb_ref, wdw_ref, bdw_ref, lng_ref, lnb_ref,
                 gconf_ref, wout_ref, lmg_ref, lmb_ref, w1_ref, w2_ref, lfg_ref, lfb_ref,
                 out_ref, ctail_out, xc_ref, zs_ref, *, n_sub, tail_at):
    i = pl.program_id(1)
    tm = h_ref.shape[1]
    rows = tm // n_sub
    subs = [slice(s * rows, (s + 1) * rows) for s in range(n_sub)]
    sb, cw = SB_WIDTH, CONF_WIDTH

    @pl.when(i == 0)
    def _():
        xc_ref[0:CONF_HALO, :] = ctail_ref[0]

    xc_ref[CONF_HALO:CONF_HALO + tm, :] = hc_ref[0]
    ctail_out[0] = xc_ref[tail_at:tail_at + CONF_HALO, :]
    z_rows = tm + CONF_HALO - SUBLANES
    for r in range(SUBLANES):
        zs_ref[r] = xc_ref[SUBLANES - r:SUBLANES - r + z_rows, :]

    def conf_chunk(t0, n):
        acc = jnp.broadcast_to(bdw_ref[...], (n, cw))
        for d in range(CONF_KERNEL):
            a, r = divmod(d, SUBLANES)
            off = t0 + CONF_HALO - SUBLANES - SUBLANES * a
            k = CONF_KERNEL - 1 - d
            acc = acc + wdw_ref[k:k + 1, :] * zs_ref[r, off:off + n, :]
        y = _layer_norm(acc, lng_ref[...], lnb_ref[...])
        y = y * jax.nn.sigmoid(y)
        return _rms_norm(y, gconf_ref[...]).astype(BF16)

    def mix(r):
        ysb = _rms_norm(o_ref[0, r, :], gsb_ref[...]).astype(BF16)
        part = _dot(ysb, wout_ref[0:sb, :]) + _dot(ys_ref[0, r, :], wout_ref[sb + cw:, :])
        yc = jnp.concatenate([conf_chunk(t0, CONV_CHUNK)
                              for t0 in range(r.start, r.stop, CONV_CHUNK)], axis=0)
        mixed = part + _dot(yc, wout_ref[sb:sb + cw, :])
        return _layer_norm(ALPHA * h_ref[0, r, :] + mixed, lmg_ref[...], lmb_ref[...])

    def ff_in(h1):
        u = jnp.maximum(_dot(h1.astype(BF16), w1_ref[...]), 0.0)
        return (u * u).astype(BF16)

    h1s = [mix(r) for r in subs]
    u2s = [ff_in(h1) for h1 in h1s]
    for r, h1, u2 in zip(subs, h1s, u2s):
        out_ref[0, r, :] = _layer_norm(ALPHA * h1 + _dot(u2, w2_ref[...]), lfg_ref[...], lfb_ref[...])

    xc_ref[0:CONF_HALO, :] = xc_ref[tm:tm + CONF_HALO, :]


def _post(h, o, hc, ys, ctail, w, *, tm, n_sub, tail_at):
    bsz, length, _ = h.shape
    row = lambda width: pl.BlockSpec((1, tm, width), lambda b, i: (b, i, 0))
    single = lambda shape: pl.BlockSpec(shape, lambda *_: (0,) * len(shape),
                                        pipeline_mode=pl.Buffered(1))
    tail_spec = lambda index_map: pl.BlockSpec((1, CONF_HALO, CONF_WIDTH), index_map)
    return pl.pallas_call(
        functools.partial(_post_kernel, n_sub=n_sub, tail_at=tail_at),
        grid=(bsz, length // tm),
        in_specs=[row(D_MODEL), row(SB_WIDTH), row(CONF_WIDTH), row(SC_WIDTH),
                  tail_spec(lambda b, i: (0, 0, 0)), _const_spec((1, SB_WIDTH)),
                  _const_spec((CONF_KERNEL, CONF_WIDTH)), _const_spec((1, CONF_WIDTH)),
                  _const_spec((1, CONF_WIDTH)), _const_spec((1, CONF_WIDTH)),
                  _const_spec((1, CONF_WIDTH)), single((D_MODEL, D_MODEL)),
                  _const_spec((1, D_MODEL)), _const_spec((1, D_MODEL)),
                  single((D_MODEL, D_FF)), single((D_FF, D_MODEL)),
                  _const_spec((1, D_MODEL)), _const_spec((1, D_MODEL))],
        out_specs=[row(D_MODEL), tail_spec(lambda b, i: (b, 0, 0))],
        out_shape=[jax.ShapeDtypeStruct((bsz, length, D_MODEL), F32),
                   jax.ShapeDtypeStruct((bsz, CONF_HALO, CONF_WIDTH), F32)],
        scratch_shapes=[pltpu.VMEM((tm + CONF_HALO, CONF_WIDTH), F32),
                        pltpu.VMEM((SUBLANES, tm + CONF_HALO - SUBLANES, CONF_WIDTH), F32)],
        compiler_params=pltpu.CompilerParams(dimension_semantics=("arbitrary", "arbitrary"),
                                             vmem_limit_bytes=VMEM_LIMIT),
        name="post",
    )(h, o, hc, ys, ctail, w["gsb"], w["wdw"], w["bdw"], w["lng"], w["lnb"], w["gconf"],
      w["wout"], w["lmg"], w["lmb"], w["w1"], w["w2"], w["lfg"], w["lfb"])


def _layer_weights(l, w_in, w_conf_dw, b_conf_dw, ln_conf_g, ln_conf_b, w_short_dw, g_mix, w_out,
                   ln_mix_g, ln_mix_b, w_ff1, w_ff2, ln_ff_g, ln_ff_b):
    wi = w_in[l]
    q_scale = LOG2E * HEAD_DIM ** -0.5
    r = lambda a: a.reshape(1, -1)
    return dict(
        wq=(wi[:, :SB_WIDTH] * q_scale).astype(BF16),
        wkt=wi[:, SB_WIDTH:2 * SB_WIDTH].T.astype(BF16),
        wv=wi[:, 2 * SB_WIDTH:3 * SB_WIDTH].astype(BF16),
        wconf=wi[:, 3 * SB_WIDTH:3 * SB_WIDTH + 2 * CONF_WIDTH].astype(BF16),
        wsc=wi[:, 3 * SB_WIDTH + 2 * CONF_WIDTH:].astype(BF16),
        wdw=w_conf_dw[l], bdw=r(b_conf_dw[l]), lng=r(ln_conf_g[l]), lnb=r(ln_conf_b[l]),
        wsh=w_short_dw[l],
        gsb=r(g_mix[l, :SB_WIDTH]), gconf=r(g_mix[l, SB_WIDTH:SB_WIDTH + CONF_WIDTH]),
        gsc=r(g_mix[l, SB_WIDTH + CONF_WIDTH:]),
        wout=w_out[l].astype(BF16), lmg=r(ln_mix_g[l]), lmb=r(ln_mix_b[l]),
        w1=w_ff1[l].astype(BF16), w2=w_ff2[l].astype(BF16), lfg=r(ln_ff_g[l]), lfb=r(ln_ff_b[l]),
    )


def kernel(x, meta_tokens, ln_in_g, ln_in_b, w_in, w_conf_dw, b_conf_dw, ln_conf_g, ln_conf_b,
           w_short_dw, g_mix, w_out, ln_mix_g, ln_mix_b, w_ff1, w_ff2, ln_ff_g, ln_ff_b):
    bsz, seq, _ = x.shape
    ln_in_gb = (ln_in_g.reshape(1, -1), ln_in_b.reshape(1, -1))
    hm = jnp.zeros((1, META_ROWS, D_MODEL), F32).at[0, :N_META].set(meta_tokens.astype(F32))
    hx = x

    rows = lax.broadcasted_iota(jnp.int32, (KV_TILE, KV_TILE), 0)
    cols = lax.broadcasted_iota(jnp.int32, (KV_TILE, KV_TILE), 1)
    ntri = -(rows >= cols).astype(BF16)
    zero_ctail = jnp.zeros((1, CONF_HALO, CONF_WIDTH), F32)
    zero_stail = jnp.zeros((1, SC_HALO, SC_WIDTH), F32)

    for l in range(DEPTH):
        w = _layer_weights(l, w_in, w_conf_dw, b_conf_dw, ln_conf_g, ln_conf_b, w_short_dw, g_mix,
                           w_out, ln_mix_g, ln_mix_b, w_ff1, w_ff2, ln_ff_g, ln_ff_b)
        first = l == 0
        qm, ktm, vm, hcm, ysm, stail, *hm0 = _proj(hm, ln_in_gb, w, zero_stail, tm=META_ROWS,
                                                   tail_at=N_META, ln_in=first)
        q, kt, v, hc, ys, _, *hx0 = _proj(hx, ln_in_gb, w, stail, tm=ROW_TILE, tail_at=ROW_TILE,
                                          ln_in=first)
        if first:
            (hm,), (hx,) = hm0, hx0
        ctx = _ctx_operands(ktm, vm)
        om = _attn(qm, ktm, vm, ctx, ntri, n_sub=1, n_ctx=0)
        hm, ctail = _post(hm, om, hcm, ysm, zero_ctail, w, tm=META_ROWS, n_sub=1, tail_at=N_META)
        o = _attn(q, kt, v, ctx, ntri, n_sub=Q_GROUPS, n_ctx=N_META)
        hx, _ = _post(hx, o, hc, ys, ctail, w, tm=ROW_TILE, n_sub=POST_SUBTILES, tail_at=ROW_TILE)
    return hx
```
